```python
import math
import jax, jax.numpy as jnp
from jax import lax
import numpy as np

D_MODEL = 1024
BATCH = 32
SEQ = 2048
DEPTH = 4

N_META = 16
CHUNK = 64
PAD = CHUNK - N_META
GLA_HEADS = 4
GLA_DK = 128
GLA_DV = 256
GLA_RANK = 16
GLA_TAU = 16.0
RET_HEADS = 4
RET_DK = 128
RET_DV = 256
ROPE_BASE = 10000.0
D_FF = 2816
N_EXPERTS = 8
TOP_K = 2
D_FF_EXPERT = 3584
ALPHA = (2 * DEPTH) ** 0.25
BETA = (8 * DEPTH) ** -0.25
LN_EPS = 1e-5
GN_EPS = 1e-6
N_DENSE = (DEPTH + 1) // 2
N_MOE = DEPTH // 2

GLA_QK = GLA_HEADS * GLA_DK
GLA_V = GLA_HEADS * GLA_DV
RET_QK = RET_HEADS * RET_DK
RET_V = RET_HEADS * RET_DV
IN_SIZES = (GLA_QK, GLA_QK, GLA_V, GLA_V, GLA_RANK,
            RET_QK, RET_QK, RET_V, RET_V,
            D_MODEL, D_MODEL)
N_IN = sum(IN_SIZES)

kernel_name = "hybrid_gla_retention_deepnorm_moe"


def layer_norm(x, g, b):
    xf = x.astype(jnp.float32)
    mu = jnp.mean(xf, axis=-1, keepdims=True)
    var = jnp.mean(jnp.square(xf - mu), axis=-1, keepdims=True)
    y = (xf - mu) * lax.rsqrt(var + LN_EPS) * g.astype(jnp.float32) + b.astype(jnp.float32)
    return y.astype(x.dtype)


def rotary(t, pos):
    half = t.shape[-1] // 2
    inv = ROPE_BASE ** (-jnp.arange(half, dtype=jnp.float32) / half)
    ang = pos[:, None] * inv[None, :]
    cos = jnp.cos(ang)[None, :, None, :]
    sin = jnp.sin(ang)[None, :, None, :]
    t1, t2 = t[..., :half], t[..., half:]
    return jnp.concatenate([t1 * cos - t2 * sin, t1 * sin + t2 * cos], axis=-1)


def to_chunks(t, n_heads):
    bsz, length, width = t.shape
    t = jnp.pad(t.astype(jnp.float32), ((0, 0), (PAD, 0), (0, 0)))
    n = (length + PAD) // CHUNK
    return t.reshape(bsz, n, CHUNK, n_heads, width // n_heads).transpose(0, 3, 1, 2, 4)


def from_chunks(o):
    bsz, h, n, c, d = o.shape
    return o.transpose(0, 2, 3, 1, 4).reshape(bsz, n * c, h, d)[:, PAD:]


def chunk_recurrence(q_in, k_tail, v, decay):
    bsz, h, n, c, dk = q_in.shape
    dv = v.shape[-1]

    def step(s, inp):
        qc, kc, vc, dc = inp
        o = jnp.einsum('bhcd,bhdv->bhcv', qc, s)
        s = s * dc[..., None] + jnp.einsum('bhcd,bhcv->bhdv', kc, vc)
        return s, o

    xs = (jnp.moveaxis(q_in, 2, 0), jnp.moveaxis(k_tail, 2, 0),
          jnp.moveaxis(v, 2, 0), jnp.moveaxis(decay, 2, 0))
    s0 = jnp.zeros((bsz, h, dk, dv), jnp.float32)
    _, o = lax.scan(step, s0, xs)
    return jnp.moveaxis(o, 0, 2)


def gla_chunked(q, k, v, log_a):
    b = jnp.cumsum(log_a, axis=3)
    b_last = b[..., -1:, :]
    q_in = q * jnp.exp(b)
    k_in = k * jnp.exp(-b)
    causal = jnp.tril(jnp.ones((CHUNK, CHUNK), dtype=bool))
    scores = jnp.where(causal, jnp.einsum('bhnid,bhnjd->bhnij', q_in, k_in), 0.0)
    o_intra = jnp.einsum('bhnij,bhnjv->bhniv', scores, v)
    k_tail = k * jnp.exp(b_last - b)
    decay = jnp.exp(b_last[..., 0, :])
    return o_intra + chunk_recurrence(q_in, k_tail, v, decay)


def retention_chunked(q, k, v, log_gamma):
    idx = jnp.arange(CHUNK, dtype=jnp.float32)
    rel = idx[:, None] - idx[None, :]
    lg = log_gamma[:, None, None]
    dmat = jnp.where(rel >= 0, jnp.exp(lg * jnp.maximum(rel, 0.0)), 0.0)
    scores = jnp.einsum('bhnid,bhnjd->bhnij', q, k) * dmat[None, :, None]
    o_intra = jnp.einsum('bhnij,bhnjv->bhniv', scores, v)
    xi = jnp.exp(log_gamma[:, None] * (idx + 1.0))[None, :, None, :, None]
    zeta = jnp.exp(log_gamma[:, None] * (CHUNK - 1.0 - idx))[None, :, None, :, None]
    bsz, h, n, _, dk = q.shape
    decay = jnp.broadcast_to(jnp.exp(log_gamma * CHUNK)[None, :, None, None], (bsz, h, n, dk))
    return o_intra + chunk_recurrence(q * xi, k * zeta, v, decay)


def token_mixer(x, w_in, b_gate, gla_w_a2, gla_b_a, gla_norm_g, ret_norm_g, ret_norm_b,
                w_pa, w_pb, w_o):
    bsz, length, _ = x.shape
    proj = x @ w_in
    split_at = np.cumsum(np.array(IN_SIZES))[:-1].tolist()
    gq, gk, gv, gr, ga, rq, rk, rv, rg, gate_a, gate_b = jnp.split(proj, split_at, axis=-1)

    log_a = jax.nn.log_sigmoid((ga @ gla_w_a2 + gla_b_a).astype(jnp.float32)) / GLA_TAU
    o_a = gla_chunked(to_chunks(gq * GLA_DK ** -0.5, GLA_HEADS), to_chunks(gk, GLA_HEADS),
                      to_chunks(gv, GLA_HEADS), to_chunks(log_a, GLA_HEADS))
    o_a = from_chunks(o_a)
    o_a = o_a * lax.rsqrt(jnp.mean(jnp.square(o_a), axis=-1, keepdims=True) + LN_EPS) \
        * gla_norm_g.astype(jnp.float32)
    y_a = (jax.nn.silu(gr.astype(jnp.float32)) * o_a.reshape(bsz, length, GLA_V)) @ w_pa

    pos = jnp.arange(length, dtype=jnp.float32)
    rq_r = rotary(rq.astype(jnp.float32).reshape(bsz, length, RET_HEADS, RET_DK), pos)
    rk_r = rotary(rk.astype(jnp.float32).reshape(bsz, length, RET_HEADS, RET_DK), pos) * RET_DK ** -0.5
    log_gamma = jnp.log(1.0 - 2.0 ** (-5.0 - jnp.arange(RET_HEADS, dtype=jnp.float32)))
    o_b = retention_chunked(to_chunks(rq_r.reshape(bsz, length, RET_QK), RET_HEADS),
                            to_chunks(rk_r.reshape(bsz, length, RET_QK), RET_HEADS),
                            to_chunks(rv, RET_HEADS), log_gamma)
    o_b = from_chunks(o_b)
    mu = jnp.mean(o_b, axis=-1, keepdims=True)
    var = jnp.mean(jnp.square(o_b - mu), axis=-1, keepdims=True)
    o_b = ((o_b - mu) * lax.rsqrt(var + GN_EPS)).reshape(bsz, length, RET_V) \
        * ret_norm_g.astype(jnp.float32) + ret_norm_b.astype(jnp.float32)
    y_b = (jax.nn.silu(rg.astype(jnp.float32)) * o_b) @ w_pb

    merged = jax.nn.sigmoid((gate_a + b_gate[0]).astype(jnp.float32)) * y_a \
        + jax.nn.sigmoid((gate_b + b_gate[1]).astype(jnp.float32)) * y_b
    return (merged @ w_o).astype(x.dtype)


def swiglu(x, w_gate, w_up, w_down):
    return (jax.nn.silu(x @ w_gate) * (x @ w_up)) @ w_down


def moe_swiglu(x, w_router, w_gate, w_up, w_down):
    bsz, length, d = x.shape
    t = x.reshape(-1, d)
    probs = jax.nn.softmax((t @ w_router).astype(jnp.float32), axis=-1)
    top_p, top_i = lax.top_k(probs, TOP_K)
    top_p = top_p / jnp.sum(top_p, axis=-1, keepdims=True)
    combine = jnp.sum(jax.nn.one_hot(top_i, N_EXPERTS, dtype=jnp.float32) * top_p[..., None], axis=1)
    out = jnp.zeros(t.shape, jnp.float32)
    for e in range(N_EXPERTS):
        out = out + combine[:, e:e + 1] * swiglu(t, w_gate[e], w_up[e], w_down[e]).astype(jnp.float32)
    return out.reshape(bsz, length, d).astype(x.dtype)


def setup_inputs(seed: int = 0) -> dict:
    key = jax.random.key(seed)
    ks = jax.random.split(key, 24)
    f32 = jnp.float32
    nrm = lambda k, shape, scale: jax.random.normal(k, shape, f32) * scale
    d = D_MODEL
    return {
        "x": nrm(ks[0], (BATCH, SEQ, d), 1.0),
        "meta": nrm(ks[1], (N_META, d), 1.0),
        "w_in": nrm(ks[2], (DEPTH, d, N_IN), d ** -0.5),
        "b_gate": nrm(ks[3], (DEPTH, 2, d), 0.01),
        "gla_w_a2": nrm(ks[4], (DEPTH, GLA_RANK, GLA_QK), GLA_RANK ** -0.5),
        "gla_b_a": nrm(ks[5], (DEPTH, GLA_QK), 0.01),
        "gla_norm_g": 1.0 + nrm(ks[6], (DEPTH, GLA_DV), 0.02),
        "ret_norm_g": 1.0 + nrm(ks[7], (DEPTH, RET_V), 0.02),
        "ret_norm_b": nrm(ks[8], (DEPTH, RET_V), 0.01),
        "w_pa": nrm(ks[9], (DEPTH, GLA_V, d), GLA_V ** -0.5),
        "w_pb": nrm(ks[10], (DEPTH, RET_V, d), RET_V ** -0.5),
        "w_o": nrm(ks[11], (DEPTH, d, d), BETA * d ** -0.5),
        "ln1_g": 1.0 + nrm(ks[12], (DEPTH, d), 0.02),
        "ln1_b": nrm(ks[13], (DEPTH, d), 0.01),
        "ffn_w_gate": nrm(ks[14], (N_DENSE, d, D_FF), d ** -0.5),
        "ffn_w_up": nrm(ks[15], (N_DENSE, d, D_FF), d ** -0.5),
        "ffn_w_down": nrm(ks[16], (N_DENSE, D_FF, d), BETA * D_FF ** -0.5),
        "moe_w_router": nrm(ks[17], (N_MOE, d, N_EXPERTS), d ** -0.5),
        "moe_w_gate": nrm(ks[18], (N_MOE, N_EXPERTS, d, D_FF_EXPERT), d ** -0.5),
        "moe_w_up": nrm(ks[19], (N_MOE, N_EXPERTS, d, D_FF_EXPERT), d ** -0.5),
        "moe_w_down": nrm(ks[20], (N_MOE, N_EXPERTS, D_FF_EXPERT, d), BETA * D_FF_EXPERT ** -0.5),
        "ln2_g": 1.0 + nrm(ks[21], (DEPTH, d), 0.02),
        "ln2_b": nrm(ks[22], (DEPTH, d), 0.01),
    }


def reference(x, meta, w_in, b_gate, gla_w_a2, gla_b_a, gla_norm_g, ret_norm_g, ret_norm_b,
              w_pa, w_pb, w_o, ln1_g, ln1_b, ffn_w_gate, ffn_w_up, ffn_w_down,
              moe_w_router, moe_w_gate, moe_w_up, moe_w_down, ln2_g, ln2_b):
    bsz = x.shape[0]
    h = jnp.concatenate([jnp.broadcast_to(meta[None].astype(x.dtype), (bsz, N_META, x.shape[-1])), x], axis=1)
    for l in range(DEPTH):
        mix = token_mixer(h, w_in[l], b_gate[l], gla_w_a2[l], gla_b_a[l], gla_norm_g[l],
                          ret_norm_g[l], ret_norm_b[l], w_pa[l], w_pb[l], w_o[l])
        h = layer_norm(ALPHA * h + mix, ln1_g[l], ln1_b[l])
        if l % 2 == 0:
            f = swiglu(h, ffn_w_gate[l // 2], ffn_w_up[l // 2], ffn_w_down[l // 2])
        else:
            f = moe_swiglu(h, moe_w_router[l // 2], moe_w_gate[l // 2], moe_w_up[l // 2], moe_w_down[l // 2])
        h = layer_norm(ALPHA * h + f, ln2_g[l], ln2_b[l])
    return h[:, N_META:]
```

```python
import functools
import math

import jax
import jax.numpy as jnp
import numpy as np
from jax import lax
from jax.experimental import pallas as pl
from jax.experimental.pallas import tpu as pltpu

F32 = jnp.float32
BF16 = jnp.bfloat16

N_META = 16
CHUNK = 64
PAD = CHUNK - N_META
HEADS = 4
DK = 128
DV = 256
GLA_RANK = 16
GLA_TAU = 16.0
ROPE_BASE = 10000.0
N_EXPERTS = 8
LN_EPS = 1e-5
GN_EPS = 1e-6

V7X_LANES = 128
V7X_BF16_SUBLANES = 16
V7X_VMEM_LIMIT_BYTES = 56 * 1024 * 1024

QK = HEADS * DK
VW = HEADS * DV
COL_GQ, COL_GK, COL_GV, COL_GR = 0, QK, 2 * QK, 2 * QK + VW
COL_RQ = 2 * QK + 2 * VW
COL_RK, COL_RV, COL_RG = COL_RQ + QK, COL_RQ + 2 * QK, COL_RQ + 2 * QK + VW
COL_GATE_A = COL_RQ + 2 * QK + 2 * VW
COL_GATE_B = COL_GATE_A + QK + QK
COL_GA = COL_GATE_B + QK + QK
N_PROJ = COL_GA + V7X_LANES


def _divisor_tile(n, cap, multiple):
    best = None
    for t in range(multiple, min(n, cap) + 1, multiple):
        if n % t == 0:
            best = t
    assert best is not None, (n, cap, multiple)
    return best


def _params(*sem):
    return pltpu.CompilerParams(dimension_semantics=sem, vmem_limit_bytes=V7X_VMEM_LIMIT_BYTES)


def _layer_norm(r, g, b):
    mu = jnp.mean(r, axis=-1, keepdims=True)
    d = r - mu
    var = jnp.mean(d * d, axis=-1, keepdims=True)
    return d * lax.rsqrt(var + LN_EPS) * g + b


def _silu(x):
    return x * jax.nn.sigmoid(x)


def _in_proj_body(x_ref, w_ref, o_ref, xb_ref):
    @pl.when(pl.program_id(1) == 0)
    def _():
        xb_ref[...] = x_ref[...].astype(BF16)

    o_ref[...] = jnp.dot(xb_ref[...], w_ref[...], preferred_element_type=F32).astype(BF16)


def _in_proj(h, w):
    m, d = h.shape
    n = w.shape[1]
    tm = _divisor_tile(m, 1056, V7X_BF16_SUBLANES)
    tn = _divisor_tile(n, 1664, V7X_LANES)
    return pl.pallas_call(
        _in_proj_body,
        grid=(m // tm, n // tn),
        in_specs=[pl.BlockSpec((tm, d), lambda i, j: (i, 0)), pl.BlockSpec((d, tn), lambda i, j: (0, j))],
        out_specs=pl.BlockSpec((tm, tn), lambda i, j: (i, j)),
        out_shape=jax.ShapeDtypeStruct((m, n), BF16),
        scratch_shapes=[pltpu.VMEM((tm, d), BF16)],
        compiler_params=_params("arbitrary", "arbitrary"),
        name="in_proj",
    )(h, w)


def _mixer_body(gq_ref, gk_ref, gv_ref, gr_ref, ga_ref, rq_ref, rk_ref, rv_ref, rg_ref,
                wa2_ref, ba_ref, gng_ref, rng_ref, rnb_ref, cos_ref, sin_ref,
                dmat_ref, xi_ref, zeta_ref, rdec_ref,
                ya_ref, yb_ref, sa_ref, sb_ref, *, n_chunks):
    c_ = CHUNK
    sa_ref[...] = jnp.zeros_like(sa_ref)
    sb_ref[...] = jnp.zeros_like(sb_ref)
    row = lax.broadcasted_iota(jnp.int32, (c_, c_), 0)
    col = lax.broadcasted_iota(jnp.int32, (c_, c_), 1)
    causal = row >= col
    tri = jnp.where(causal, 1.0, 0.0).astype(BF16)
    wa2 = wa2_ref[...]
    ba = ba_ref[...]
    gng = gng_ref[...]
    rng = rng_ref[...]
    rnb = rnb_ref[...]
    dmat = dmat_ref[0]
    xi = xi_ref[0]
    zeta = zeta_ref[0]
    rdec = rdec_ref[0]
    nt = (((1,), (1,)), ((), ()))
    tn = (((0,), (0,)), ((), ()))

    def chunk(c, carry):
        rows = pl.ds(pl.multiple_of(c * c_, c_), c_)
        ridx = c * c_ + lax.broadcasted_iota(jnp.int32, (c_, 1), 0)
        valid = ridx >= PAD

        z = jnp.dot(ga_ref[0, rows, :], wa2, preferred_element_type=F32) + ba
        log_sig = jnp.minimum(z, 0.0) - jnp.log1p(jnp.exp(-jnp.abs(z)))
        la = jnp.where(valid, log_sig * (1.0 / GLA_TAU), 0.0)
        la_hi = la.astype(BF16)
        la_lo = (la - la_hi.astype(F32)).astype(BF16)
        b = (jnp.dot(tri, la_hi, preferred_element_type=F32)
             + jnp.dot(tri, la_lo, preferred_element_type=F32))
        b_last = b[c_ - 1:c_, :]
        q = gq_ref[0, rows, :].astype(F32) * (DK ** -0.5)
        k = jnp.where(valid, gk_ref[0, rows, :].astype(F32), 0.0)
        v = jnp.where(valid, gv_ref[0, rows, :], jnp.zeros((), BF16))
        q_in = (q * jnp.exp(b)).astype(BF16)
        k_in = (k * jnp.exp(-b)).astype(BF16)
        k_tail = (k * jnp.exp(b_last - b)).astype(BF16)
        dec = jnp.exp(b_last)
        s = lax.dot_general(q_in, k_in, nt, preferred_element_type=F32)
        s = jnp.where(causal, s, 0.0).astype(BF16)
        st = sa_ref[...]
        o = (jnp.dot(s, v, preferred_element_type=F32)
             + jnp.dot(q_in, st.astype(BF16), preferred_element_type=F32))
        dcol = jnp.transpose(jnp.broadcast_to(dec, (DK, DK)))
        dfull = jnp.concatenate([dcol, dcol], axis=1)
        sa_ref[...] = st * dfull + lax.dot_general(k_tail, v, tn, preferred_element_type=F32)
        ms = jnp.mean(o * o, axis=-1, keepdims=True)
        on = o * lax.rsqrt(ms + LN_EPS) * gng
        ya_ref[0, rows, :] = (_silu(gr_ref[0, rows, :].astype(F32)) * on).astype(BF16)

        cos = cos_ref[rows, :]
        sin = sin_ref[rows, :]
        rq = rq_ref[0, rows, :].astype(F32)
        rk = rk_ref[0, rows, :].astype(F32)
        qr = rq * cos + pltpu.roll(rq, DK // 2, 1) * sin
        kr = jnp.where(valid, rk * cos + pltpu.roll(rk, DK // 2, 1) * sin, 0.0) * (DK ** -0.5)
        vb = jnp.where(valid, rv_ref[0, rows, :], jnp.zeros((), BF16))
        sc = lax.dot_general(qr.astype(BF16), kr.astype(BF16), nt, preferred_element_type=F32) * dmat
        stb = sb_ref[...]
        ob = (jnp.dot(sc.astype(BF16), vb, preferred_element_type=F32)
              + jnp.dot((qr * xi).astype(BF16), stb.astype(BF16), preferred_element_type=F32))
        sb_ref[...] = stb * rdec + lax.dot_general((kr * zeta).astype(BF16), vb, tn, preferred_element_type=F32)
        mu = jnp.mean(ob, axis=-1, keepdims=True)
        dd = ob - mu
        var = jnp.mean(dd * dd, axis=-1, keepdims=True)
        obn = dd * lax.rsqrt(var + GN_EPS) * rng + rnb
        yb_ref[0, rows, :] = (_silu(rg_ref[0, rows, :].astype(F32)) * obn).astype(BF16)
        return carry

    lax.fori_loop(0, n_chunks, chunk, 0)


def _retention_tables(lp):
    half = DK // 2
    pos = np.arange(lp, dtype=np.float64) - PAD
    inv = ROPE_BASE ** (-np.arange(half, dtype=np.float64) / half)
    ang = pos[:, None] * inv[None, :]
    cos = np.concatenate([np.cos(ang), np.cos(ang)], axis=1)
    sin = np.concatenate([-np.sin(ang), np.sin(ang)], axis=1)
    lg = np.log(1.0 - 2.0 ** (-5.0 - np.arange(HEADS, dtype=np.float64)))
    idx = np.arange(CHUNK, dtype=np.float64)
    rel = idx[:, None] - idx[None, :]
    dmat = np.where(rel >= 0, np.exp(lg[:, None, None] * np.maximum(rel, 0.0)), 0.0)
    xi = np.broadcast_to(np.exp(lg[:, None] * (idx + 1.0))[:, :, None], (HEADS, CHUNK, DK))
    zeta = np.broadcast_to(np.exp(lg[:, None] * (CHUNK - 1.0 - idx))[:, :, None], (HEADS, CHUNK, DK))
    rdec = np.broadcast_to(np.exp(lg * CHUNK)[:, None, None], (HEADS, 1, DV))
    f = lambda a: jnp.asarray(np.ascontiguousarray(a), F32)
    return f(cos), f(sin), f(dmat), f(xi), f(zeta), f(rdec)


def _mixer(proj3, wa2, ba, gng, rng, rnb, tables):
    bsz, lp, _ = proj3.shape
    cos, sin, dmat, xi, zeta, rdec = tables
    qk = lambda col: pl.BlockSpec((1, lp, DK), lambda b, h, c=col // DK: (b, 0, c + h))
    vv = lambda col: pl.BlockSpec((1, lp, DV), lambda b, h, c=col // DV: (b, 0, c + h))
    in_specs = [
        qk(COL_GQ), qk(COL_GK), vv(COL_GV), vv(COL_GR),
        pl.BlockSpec((1, lp, V7X_LANES), lambda b, h: (b, 0, COL_GA // V7X_LANES)),
        qk(COL_RQ), qk(COL_RK), vv(COL_RV), vv(COL_RG),
        pl.BlockSpec((V7X_LANES, DK), lambda b, h: (0, h)),
        pl.BlockSpec((1, DK), lambda b, h: (0, h)),
        pl.BlockSpec((1, DV), lambda b, h: (0, 0)),
        pl.BlockSpec((1, DV), lambda b, h: (0, h)),
        pl.BlockSpec((1, DV), lambda b, h: (0, h)),
        pl.BlockSpec((lp, DK), lambda b, h: (0, 0)),
        pl.BlockSpec((lp, DK), lambda b, h: (0, 0)),
        pl.BlockSpec((1, CHUNK, CHUNK), lambda b, h: (h, 0, 0)),
        pl.BlockSpec((1, CHUNK, DK), lambda b, h: (h, 0, 0)),
        pl.BlockSpec((1, CHUNK, DK), lambda b, h: (h, 0, 0)),
        pl.BlockSpec((1, 1, DV), lambda b, h: (h, 0, 0)),
    ]
    out_spec = pl.BlockSpec((1, lp, DV), lambda b, h: (b, 0, h))
    return pl.pallas_call(
        functools.partial(_mixer_body, n_chunks=lp // CHUNK),
        grid=(bsz, HEADS),
        in_specs=in_specs,
        out_specs=[out_spec, out_spec],
        out_shape=[jax.ShapeDtypeStruct((bsz, lp, VW), BF16)] * 2,
        scratch_shapes=[pltpu.VMEM((DK, DV), F32), pltpu.VMEM((DK, DV), F32)],
        compiler_params=_params("arbitrary", "arbitrary"),
        name="mixer",
    )(*([proj3] * 4), proj3, *([proj3] * 4), wa2, ba, gng, rng, rnb, cos, sin, dmat, xi, zeta, rdec)


def _mix_out_body(ya_ref, yb_ref, gta_ref, gtb_ref, h_ref, wpa_ref, wpb_ref, wo_ref,
                  bga_ref, bgb_ref, g_ref, b_ref, o_ref, *, alpha):
    y_a = jnp.dot(ya_ref[...], wpa_ref[...], preferred_element_type=F32)
    y_b = jnp.dot(yb_ref[...], wpb_ref[...], preferred_element_type=F32)
    merged = (jax.nn.sigmoid(gta_ref[...].astype(F32) + bga_ref[...]) * y_a
              + jax.nn.sigmoid(gtb_ref[...].astype(F32) + bgb_ref[...]) * y_b)
    mix = jnp.dot(merged.astype(BF16), wo_ref[...], preferred_element_type=F32)
    o_ref[...] = _layer_norm(alpha * h_ref[...] + mix, g_ref[...], b_ref[...])


def _mix_out(ya, yb, proj, h, wpa, wpb, wo, bga, bgb, g, b, alpha):
    m, d = h.shape
    tm = _divisor_tile(m, 528, V7X_BF16_SUBLANES)
    rows = lambda: pl.BlockSpec((tm, d), lambda i: (i, 0))
    full = lambda r: pl.BlockSpec((r, d), lambda i: (0, 0))
    return pl.pallas_call(
        functools.partial(_mix_out_body, alpha=alpha),
        grid=(m // tm,),
        in_specs=[rows(), rows(),
                  pl.BlockSpec((tm, d), lambda i: (i, COL_GATE_A // d)),
                  pl.BlockSpec((tm, d), lambda i: (i, COL_GATE_B // d)),
                  rows(), full(d), full(d), full(d), full(1), full(1), full(1), full(1)],
        out_specs=rows(),
        out_shape=jax.ShapeDtypeStruct((m, d), F32),
        compiler_params=_params("arbitrary"),
        name="mix_out",
    )(ya, yb, proj, proj, h, wpa, wpb, wo, bga, bgb, g, b)


def _ffn_body(x_ref, wg_ref, wu_ref, wd_ref, g_ref, b_ref, o_ref, xb_ref, acc_ref, *, alpha):
    f = pl.program_id(1)

    @pl.when(f == 0)
    def _():
        xb_ref[...] = x_ref[...].astype(BF16)
        acc_ref[...] = jnp.zeros_like(acc_ref)

    xb = xb_ref[...]
    hid = (_silu(jnp.dot(xb, wg_ref[...], preferred_element_type=F32))
           * jnp.dot(xb, wu_ref[...], preferred_element_type=F32))
    acc_ref[...] += jnp.dot(hid.astype(BF16), wd_ref[...], preferred_element_type=F32)

    @pl.when(f == pl.num_programs(1) - 1)
    def _():
        o_ref[...] = _layer_norm(alpha * x_ref[...] + acc_ref[...], g_ref[...], b_ref[...])


def _ffn(h, wg, wu, wd, g, b, alpha):
    m, d = h.shape
    dff = wg.shape[1]
    tm = _divisor_tile(m, 528, V7X_BF16_SUBLANES)
    tf = _divisor_tile(dff, 1408, V7X_LANES)
    return pl.pallas_call(
        functools.partial(_ffn_body, alpha=alpha),
        grid=(m // tm, dff // tf),
        in_specs=[pl.BlockSpec((tm, d), lambda i, f: (i, 0)),
                  pl.BlockSpec((d, tf), lambda i, f: (0, f)),
                  pl.BlockSpec((d, tf), lambda i, f: (0, f)),
                  pl.BlockSpec((tf, d), lambda i, f: (f, 0)),
                  pl.BlockSpec((1, d), lambda i, f: (0, 0)),
                  pl.BlockSpec((1, d), lambda i, f: (0, 0))],
        out_specs=pl.BlockSpec((tm, d), lambda i, f: (i, 0)),
        out_shape=jax.ShapeDtypeStruct((m, d), F32),
        scratch_shapes=[pltpu.VMEM((tm, d), BF16), pltpu.VMEM((tm, d), F32)],
        compiler_params=_params("arbitrary", "arbitrary"),
        name="ffn",
    )(h, wg, wu, wd, g, b)


def _router_body(h_ref, wr_ref, idx_ref, wt_ref):
    logits = lax.dot_general(wr_ref[...], h_ref[...], (((1,), (1,)), ((), ())),
                             precision=lax.Precision.HIGHEST, preferred_element_type=F32)
    e = lax.broadcasted_iota(jnp.int32, logits.shape, 0)
    m1 = jnp.max(logits, axis=0, keepdims=True)
    i1 = jnp.min(jnp.where(logits == m1, e, N_EXPERTS), axis=0, keepdims=True)
    rest = jnp.where(e == i1, -jnp.inf, logits)
    m2 = jnp.max(rest, axis=0, keepdims=True)
    i2 = jnp.min(jnp.where(rest == m2, e, N_EXPERTS), axis=0, keepdims=True)
    e2 = jnp.exp(m2 - m1)
    w1 = 1.0 / (1.0 + e2)
    idx_ref[0:1, :] = i1
    idx_ref[1:2, :] = i2
    wt_ref[0:1, :] = w1
    wt_ref[1:2, :] = e2 * w1


def _router(h, wr_t):
    m, d = h.shape
    tm = _divisor_tile(m, 1024, V7X_LANES)
    return pl.pallas_call(
        _router_body,
        grid=(m // tm,),
        in_specs=[pl.BlockSpec((tm, d), lambda i: (i, 0)), pl.BlockSpec((N_EXPERTS, d), lambda i: (0, 0))],
        out_specs=[pl.BlockSpec((2, tm), lambda i: (0, i)), pl.BlockSpec((2, tm), lambda i: (0, i))],
        out_shape=[jax.ShapeDtypeStruct((2, m), jnp.int32), jax.ShapeDtypeStruct((2, m), F32)],
        compiler_params=_params("arbitrary"),
        name="router",
    )(h, wr_t)


def _index_prefetch(idx_hbm, idx_smem, isem, i, n):
    slot = lax.rem(i, 2)

    @pl.when(i == 0)
    def _():
        pltpu.make_async_copy(idx_hbm.at[0], idx_smem.at[0], isem.at[0]).start()

    pltpu.make_async_copy(idx_hbm.at[i], idx_smem.at[slot], isem.at[slot]).wait()

    @pl.when(i + 1 < n)
    def _():
        pltpu.make_async_copy(idx_hbm.at[i + 1], idx_smem.at[1 - slot], isem.at[1 - slot]).start()

    return slot


def _gather_body(src_hbm, h_hbm, o_ref, idx_smem, isem, sem, *, g):
    i = pl.program_id(0)
    slot = _index_prefetch(src_hbm, idx_smem, isem, i, pl.num_programs(0))

    def issue(r, carry):
        s = idx_smem[slot, r]
        pltpu.make_async_copy(h_hbm.at[pl.ds(s, 1), :], o_ref.at[pl.ds(r, 1), :], sem).start()
        return carry

    lax.fori_loop(0, g, issue, 0)
    pltpu.make_async_copy(h_hbm.at[pl.ds(0, g), :], o_ref, sem).wait()


def _gather_rows(h, src2d):
    nt, g = src2d.shape
    d = h.shape[1]
    return pl.pallas_call(
        functools.partial(_gather_body, g=g),
        grid=(nt,),
        in_specs=[pl.BlockSpec(memory_space=pl.ANY), pl.BlockSpec(memory_space=pl.ANY)],
        out_specs=pl.BlockSpec((g, d), lambda i: (i, 0)),
        out_shape=jax.ShapeDtypeStruct((nt * g, d), h.dtype),
        scratch_shapes=[pltpu.SMEM((2, g), jnp.int32), pltpu.SemaphoreType.DMA((2,)), pltpu.SemaphoreType.DMA],
        compiler_params=_params("arbitrary"),
        name="moe_gather",
    )(src2d, h)


def _moe_body(te_ref, nu_ref, x_ref, wg_ref, wu_ref, wd_ref, o_ref, xb_ref, acc_ref):
    i = pl.program_id(0)
    f = pl.program_id(1)
    used = i < nu_ref[0]

    @pl.when(f == 0)
    def _():
        xb_ref[...] = x_ref[...].astype(BF16)
        acc_ref[...] = jnp.zeros_like(acc_ref)

    @pl.when(used)
    def _():
        xb = xb_ref[...]
        hid = (_silu(jnp.dot(xb, wg_ref[0], preferred_element_type=F32))
               * jnp.dot(xb, wu_ref[0], preferred_element_type=F32))
        acc_ref[...] += jnp.dot(hid.astype(BF16), wd_ref[0], preferred_element_type=F32)

    @pl.when(f == pl.num_programs(1) - 1)
    def _():
        o_ref[...] = acc_ref[...]


def _moe_experts(xs, tile_expert, n_used, wg, wu, wd, tme):
    r, d = xs.shape
    dff = wg.shape[2]
    tf = _divisor_tile(dff, 896, V7X_LANES)
    grid_spec = pltpu.PrefetchScalarGridSpec(
        num_scalar_prefetch=2,
        grid=(r // tme, dff // tf),
        in_specs=[pl.BlockSpec((tme, d), lambda i, f, te, nu: (i, 0)),
                  pl.BlockSpec((1, d, tf), lambda i, f, te, nu: (te[i], 0, f)),
                  pl.BlockSpec((1, d, tf), lambda i, f, te, nu: (te[i], 0, f)),
                  pl.BlockSpec((1, tf, d), lambda i, f, te, nu: (te[i], f, 0))],
        out_specs=pl.BlockSpec((tme, d), lambda i, f, te, nu: (i, 0)),
        scratch_shapes=[pltpu.VMEM((tme, d), BF16), pltpu.VMEM((tme, d), F32)],
    )
    return pl.pallas_call(
        _moe_body,
        grid_spec=grid_spec,
        out_shape=jax.ShapeDtypeStruct((r, d), F32),
        compiler_params=_params("arbitrary", "arbitrary"),
        name="moe_experts",
    )(tile_expert, n_used, xs, wg, wu, wd)


def _combine_body(pos_hbm, ys_hbm, wt_ref, h_ref, g_ref, b_ref, o_ref, ybuf, idx_smem, isem, sem, *, tm, alpha):
    i = pl.program_id(0)
    slot = _index_prefetch(pos_hbm, idx_smem, isem, i, pl.num_programs(0))

    def issue(r, carry):
        p0 = idx_smem[slot, r]
        p1 = idx_smem[slot, tm + r]
        pltpu.make_async_copy(ys_hbm.at[pl.ds(p0, 1), :], ybuf.at[0, pl.ds(r, 1), :], sem).start()
        pltpu.make_async_copy(ys_hbm.at[pl.ds(p1, 1), :], ybuf.at[1, pl.ds(r, 1), :], sem).start()
        return carry

    lax.fori_loop(0, tm, issue, 0)
    pltpu.make_async_copy(ys_hbm.at[pl.ds(0, tm), :], ybuf.at[0], sem).wait()
    pltpu.make_async_copy(ys_hbm.at[pl.ds(0, tm), :], ybuf.at[1], sem).wait()
    wt = wt_ref[...]
    out = wt[:, 0:1] * ybuf[0] + wt[:, 1:2] * ybuf[1]
    o_ref[...] = _layer_norm(alpha * h_ref[...] + out, g_ref[...], b_ref[...])


def _combine(pos2d, ys, wt, h, g, b, alpha, tm):
    m, d = h.shape
    return pl.pallas_call(
        functools.partial(_combine_body, tm=tm, alpha=alpha),
        grid=(m // tm,),
        in_specs=[pl.BlockSpec(memory_space=pl.ANY), pl.BlockSpec(memory_space=pl.ANY),
                  pl.BlockSpec((tm, 2), lambda i: (i, 0)),
                  pl.BlockSpec((tm, d), lambda i: (i, 0)),
                  pl.BlockSpec((1, d), lambda i: (0, 0)), pl.BlockSpec((1, d), lambda i: (0, 0))],
        out_specs=pl.BlockSpec((tm, d), lambda i: (i, 0)),
        out_shape=jax.ShapeDtypeStruct((m, d), F32),
        scratch_shapes=[pltpu.VMEM((2, tm, d), F32), pltpu.SMEM((2, 2 * tm), jnp.int32),
                        pltpu.SemaphoreType.DMA((2,)), pltpu.SemaphoreType.DMA],
        compiler_params=_params("arbitrary"),
        name="moe_combine",
    )(pos2d, ys, wt, h, g, b)


def _moe_layer(h, wr, wg, wu, wd, g, b, alpha):
    m, d = h.shape
    tme = _divisor_tile(m, 1024, V7X_LANES)
    tmc = _divisor_tile(m, 512, 8)
    idx, wts = _router(h, wr.T)

    e_flat = idx.reshape(-1)
    onehot = (e_flat[:, None] == jnp.arange(N_EXPERTS, dtype=jnp.int32)[None, :]).astype(jnp.int32)
    csum = jnp.cumsum(onehot, axis=0)
    rank = jnp.take_along_axis(csum, e_flat[:, None], axis=1)[:, 0] - 1
    counts = csum[-1]
    padded = ((counts + tme - 1) // tme) * tme
    ends = jnp.cumsum(padded)
    starts = ends - padded
    pos = starts[e_flat] + rank
    n_rows = 2 * m + N_EXPERTS * tme
    tok = jnp.tile(jnp.arange(m, dtype=jnp.int32), 2)
    src = jnp.zeros((n_rows,), jnp.int32).at[pos].set(tok)
    n_tiles = n_rows // tme
    tile_start = jnp.arange(n_tiles, dtype=jnp.int32) * tme
    tile_expert = jnp.minimum(jnp.sum(tile_start[:, None] >= ends[None, :], axis=1), N_EXPERTS - 1).astype(jnp.int32)
    n_used = (ends[-1] // tme).astype(jnp.int32).reshape(1)

    xs = _gather_rows(h, src.reshape(n_tiles, tme))
    ys = _moe_experts(xs, tile_expert, n_used, wg, wu, wd, tme)
    pos2d = pos.reshape(2, m // tmc, tmc).transpose(1, 0, 2).reshape(m // tmc, 2 * tmc)
    return _combine(pos2d, ys, wts.T, h, g, b, alpha, tmc)


def kernel(x, meta, w_in, b_gate, gla_w_a2, gla_b_a, gla_norm_g, ret_norm_g, ret_norm_b, w_pa, w_pb, w_o,
           ln1_g, ln1_b, ffn_w_gate, ffn_w_up, ffn_w_down, moe_w_router, moe_w_gate, moe_w_up, moe_w_down,
           ln2_g, ln2_b):
    bsz, seq, d = x.shape
    depth = w_in.shape[0]
    alpha = (2 * depth) ** 0.25
    lp = seq + N_META + PAD
    assert lp % CHUNK == 0 and d == VW
    m = bsz * lp

    h = jnp.concatenate([jnp.zeros((bsz, PAD, d), x.dtype),
                         jnp.broadcast_to(meta[None].astype(x.dtype), (bsz, N_META, d)), x], axis=1)
    h = h.reshape(m, d)
    tables = _retention_tables(lp)

    c0 = 2 * QK + 2 * VW
    row = lambda a: a.reshape(1, -1)
    for l in range(depth):
        w = w_in[l]
        w_perm = jnp.concatenate(
            [w[:, :c0], w[:, c0 + GLA_RANK:], w[:, c0:c0 + GLA_RANK],
             jnp.zeros((d, V7X_LANES - GLA_RANK), w.dtype)], axis=1).astype(BF16)
        wa2 = jnp.concatenate([gla_w_a2[l], jnp.zeros((V7X_LANES - GLA_RANK, QK), F32)], axis=0).astype(BF16)

        proj = _in_proj(h, w_perm)
        ya, yb = _mixer(proj.reshape(bsz, lp, N_PROJ), wa2, row(gla_b_a[l]), row(gla_norm_g[l]),
                        row(ret_norm_g[l]), row(ret_norm_b[l]), tables)
        h = _mix_out(ya.reshape(m, VW), yb.reshape(m, VW), proj, h,
                     w_pa[l].astype(BF16), w_pb[l].astype(BF16), w_o[l].astype(BF16),
                     row(b_gate[l, 0]), row(b_gate[l, 1]), row(ln1_g[l]), row(ln1_b[l]), alpha)
        if l % 2 == 0:
            j = l // 2
            h = _ffn(h, ffn_w_gate[j].astype(BF16), ffn_w_up[j].astype(BF16), ffn_w_down[j].astype(BF16),
                     row(ln2_g[l]), row(ln2_b[l]), alpha)
        else:
            j = l // 2
            h = _moe_layer(h, moe_w_router[j], moe_w_gate[j].astype(BF16), moe_w_up[j].astype(BF16),
                           moe_w_down[j].astype(BF16), row(ln2_g[l]), row(ln2_b[l]), alpha)
    return h.reshape(bsz, lp, d)[:, PAD + N_META:]
```

```python
import functools
import math

import jax
import jax.numpy as jnp
import numpy as np
from jax import lax
from jax.experimental import pallas as pl
from jax.experimental.pallas import tpu as pltpu

F32 = jnp.float32
BF16 = jnp.bfloat16

N_META = 16
CHUNK = 64
PAD = CHUNK - N_META
HEADS = 4
DK = 128
DV = 256
GLA_RANK = 16
GLA_TAU = 16.0
ROPE_BASE = 10000.0
N_EXPERTS = 8
LN_EPS = 1e-5
GN_EPS = 1e-6

V7X_LANES = 128
V7X_BF16_SUBLANES = 16
V7X_VMEM_LIMIT_BYTES = 56 * 1024 * 1024

QK = HEADS * DK
VW = HEADS * DV
COL_GQ, COL_GK, COL_GV, COL_GR = 0, QK, 2 * QK, 2 * QK + VW
COL_RQ = 2 * QK + 2 * VW
COL_RK, COL_RV, COL_RG = COL_RQ + QK, COL_RQ + 2 * QK, COL_RQ + 2 * QK + VW
COL_GATE_A = COL_RQ + 2 * QK + 2 * VW
COL_GATE_B = COL_GATE_A + QK + QK
COL_GA = COL_GATE_B + QK + QK
N_PROJ = COL_GA + V7X_LANES


def _divisor_tile(n, cap, multiple):
    best = None
    for t in range(multiple, min(n, cap) + 1, multiple):
        if n % t == 0:
            best = t
    assert best is not None, (n, cap, multiple)
    return best


def _params(*sem):
    return pltpu.CompilerParams(dimension_semantics=sem, vmem_limit_bytes=V7X_VMEM_LIMIT_BYTES)


def _layer_norm(r, g, b):
    mu = jnp.mean(r, axis=-1, keepdims=True)
    d = r - mu
    var = jnp.mean(d * d, axis=-1, keepdims=True)
    return d * lax.rsqrt(var + LN_EPS) * g + b


def _silu(x):
    return x * jax.nn.sigmoid(x)


def _in_proj_body(x_ref, keep_ref, w_ref, o_ref, xb_ref):
    @pl.when(pl.program_id(1) == 0)
    def _():
        xb_ref[...] = jnp.where(keep_ref[...] > 0.0, x_ref[...], 0.0).astype(BF16)

    o_ref[...] = jnp.dot(xb_ref[...], w_ref[...], preferred_element_type=F32).astype(BF16)


def _in_proj(h, keep, w):
    m, d = h.shape
    n = w.shape[1]
    tm = _divisor_tile(m, 1056, V7X_BF16_SUBLANES)
    tn = _divisor_tile(n, 1664, V7X_LANES)
    return pl.pallas_call(
        _in_proj_body,
        grid=(m // tm, n // tn),
        in_specs=[pl.BlockSpec((tm, d), lambda i, j: (i, 0)), pl.BlockSpec((tm, 1), lambda i, j: (i, 0)),
                  pl.BlockSpec((d, tn), lambda i, j: (0, j))],
        out_specs=pl.BlockSpec((tm, tn), lambda i, j: (i, j)),
        out_shape=jax.ShapeDtypeStruct((m, n), BF16),
        scratch_shapes=[pltpu.VMEM((tm, d), BF16)],
        compiler_params=_params("arbitrary", "arbitrary"),
        name="in_proj",
    )(h, keep, w)


def _mixer_body(gq_ref, gk_ref, gv_ref, gr_ref, ga_ref, rq_ref, rk_ref, rv_ref, rg_ref,
                wa2_ref, ba_ref, gng_ref, rng_ref, rnb_ref, cos_ref, sin_ref,
                dmat_ref, xi_ref, zeta_ref, rdec_ref,
                ya_ref, yb_ref, sa_ref, sb_ref, b_ref, oa_ref, ob_ref, *, n_groups, gsz):
    c_ = CHUNK
    r_ = gsz * c_
    sa_ref[...] = jnp.zeros_like(sa_ref)
    sb_ref[...] = jnp.zeros_like(sb_ref)
    row = lax.broadcasted_iota(jnp.int32, (c_, c_), 0)
    col = lax.broadcasted_iota(jnp.int32, (c_, c_), 1)
    causal = row >= col
    grow = lax.broadcasted_iota(jnp.int32, (r_, r_), 0)
    gcol = lax.broadcasted_iota(jnp.int32, (r_, r_), 1)
    shift = int(math.log2(c_))
    gtri = jnp.where((jnp.right_shift(grow, shift) == jnp.right_shift(gcol, shift)) & (grow >= gcol),
                     1.0, 0.0).astype(BF16)
    wa2 = wa2_ref[...]
    ba = ba_ref[...]
    gng = gng_ref[...]
    rng = rng_ref[...]
    rnb = rnb_ref[...]
    dmat = dmat_ref[0]
    xi = xi_ref[0]
    zeta = zeta_ref[0]
    rdec = rdec_ref[0]
    nt = (((1,), (1,)), ((), ()))
    tn = (((0,), (0,)), ((), ()))
    mm = functools.partial(jnp.dot, preferred_element_type=F32)
    mmg = functools.partial(lax.dot_general, preferred_element_type=F32)
    chunks = [slice(g * c_, (g + 1) * c_) for g in range(gsz)]

    def group_rows(j):
        return pl.ds(pl.multiple_of(j * r_, c_), r_)

    def gate_logits(j):
        return mm(ga_ref[0, group_rows(j), :], wa2) + ba

    def log_decay(z, j):
        valid = j * r_ + lax.broadcasted_iota(jnp.int32, (r_, 1), 0) >= PAD
        log_sig = jnp.minimum(z, 0.0) - jnp.log1p(jnp.exp(-jnp.abs(z)))
        la = jnp.where(valid, log_sig * (1.0 / GLA_TAU), 0.0)
        la_hi = la.astype(BF16)
        la_lo = (la - la_hi.astype(F32)).astype(BF16)
        return mm(gtri, la_hi) + mm(gtri, la_lo)

    def finish(j, slot):
        rows = group_rows(j)
        o = oa_ref[slot]
        ms = jnp.mean(o * o, axis=-1, keepdims=True)
        on = o * lax.rsqrt(ms + LN_EPS) * gng
        ya_ref[0, rows, :] = (_silu(gr_ref[0, rows, :].astype(F32)) * on).astype(BF16)
        ob = ob_ref[slot]
        mu = jnp.mean(ob, axis=-1, keepdims=True)
        dd = ob - mu
        var = jnp.mean(dd * dd, axis=-1, keepdims=True)
        obn = dd * lax.rsqrt(var + GN_EPS) * rng + rnb
        yb_ref[0, rows, :] = (_silu(rg_ref[0, rows, :].astype(F32)) * obn).astype(BF16)

    oa_ref[...] = jnp.zeros_like(oa_ref)
    ob_ref[...] = jnp.zeros_like(ob_ref)
    b_ref[0] = log_decay(gate_logits(0), 0)

    def group(gi, carry):
        slot = lax.rem(gi, 2)
        nxt = jnp.minimum(gi + 1, n_groups - 1)
        z_next = gate_logits(nxt)

        rows = group_rows(gi)
        cos = cos_ref[rows, :]
        sin = sin_ref[rows, :]
        rq = rq_ref[0, rows, :].astype(F32)
        rk = rk_ref[0, rows, :].astype(F32)
        qr = rq * cos + pltpu.roll(rq, DK // 2, 1) * sin
        kr = (rk * cos + pltpu.roll(rk, DK // 2, 1) * sin) * (DK ** -0.5)
        vb = rv_ref[0, rows, :]
        qr_b = qr.astype(BF16)
        kr_b = kr.astype(BF16)
        s_b = [mmg(qr_b[cs], kr_b[cs], nt) for cs in chunks]
        u_b = [mmg((kr[cs] * zeta).astype(BF16), vb[cs], tn) for cs in chunks]

        b = b_ref[slot]
        q = gq_ref[0, rows, :].astype(F32) * (DK ** -0.5)
        k = gk_ref[0, rows, :].astype(F32)
        v = gv_ref[0, rows, :]
        q_in = (q * jnp.exp(b)).astype(BF16)
        k_in = (k * jnp.exp(-b)).astype(BF16)
        s_a = [mmg(q_in[cs], k_in[cs], nt) for cs in chunks]
        u_a, dfull = [], []
        for cs in chunks:
            b_c = b[cs]
            b_last = b_c[c_ - 1:c_, :]
            k_tail = (k[cs] * jnp.exp(b_last - b_c)).astype(BF16)
            u_a.append(mmg(k_tail, v[cs], tn))
            dcol = jnp.transpose(jnp.broadcast_to(jnp.exp(b_last), (DK, DK)))
            dfull.append(jnp.concatenate([dcol, dcol], axis=1))

        b_ref[1 - slot] = log_decay(z_next, nxt)
        finish(jnp.maximum(gi - 1, 0), 1 - slot)

        st_a = [sa_ref[...]]
        st_b = [sb_ref[...]]
        for g in range(gsz):
            st_a.append(st_a[g] * dfull[g] + u_a[g])
            st_b.append(st_b[g] * rdec + u_b[g])
        sa_ref[...] = st_a[gsz]
        sb_ref[...] = st_b[gsz]

        o_a, o_b = [], []
        for g, cs in enumerate(chunks):
            o_b.append(mm((s_b[g] * dmat).astype(BF16), vb[cs])
                       + mm((qr[cs] * xi).astype(BF16), st_b[g].astype(BF16)))
        for g, cs in enumerate(chunks):
            sm = jnp.where(causal, s_a[g], 0.0).astype(BF16)
            o_a.append(mm(sm, v[cs]) + mm(q_in[cs], st_a[g].astype(BF16)))
        oa_ref[slot] = jnp.concatenate(o_a, axis=0)
        ob_ref[slot] = jnp.concatenate(o_b, axis=0)
        return carry

    lax.fori_loop(0, n_groups, group, 0)
    finish(n_groups - 1, (n_groups - 1) % 2)


def _retention_tables(lp):
    half = DK // 2
    pos = np.arange(lp, dtype=np.float64) - PAD
    inv = ROPE_BASE ** (-np.arange(half, dtype=np.float64) / half)
    ang = pos[:, None] * inv[None, :]
    cos = np.concatenate([np.cos(ang), np.cos(ang)], axis=1)
    sin = np.concatenate([-np.sin(ang), np.sin(ang)], axis=1)
    lg = np.log(1.0 - 2.0 ** (-5.0 - np.arange(HEADS, dtype=np.float64)))
    idx = np.arange(CHUNK, dtype=np.float64)
    rel = idx[:, None] - idx[None, :]
    dmat = np.where(rel >= 0, np.exp(lg[:, None, None] * np.maximum(rel, 0.0)), 0.0)
    xi = np.broadcast_to(np.exp(lg[:, None] * (idx + 1.0))[:, :, None], (HEADS, CHUNK, DK))
    zeta = np.broadcast_to(np.exp(lg[:, None] * (CHUNK - 1.0 - idx))[:, :, None], (HEADS, CHUNK, DK))
    rdec = np.broadcast_to(np.exp(lg * CHUNK)[:, None, None], (HEADS, 1, DV))
    f = lambda a: jnp.asarray(np.ascontiguousarray(a), F32)
    return f(cos), f(sin), f(dmat), f(xi), f(zeta), f(rdec)


def _mixer(proj3, wa2, ba, gng, rng, rnb, tables):
    bsz, lp, _ = proj3.shape
    cos, sin, dmat, xi, zeta, rdec = tables
    n_chunks = lp // CHUNK
    gsz = 3 if n_chunks % 3 == 0 else 1
    qk = lambda col: pl.BlockSpec((1, lp, DK), lambda b, h, c=col // DK: (b, 0, c + h))
    vv = lambda col: pl.BlockSpec((1, lp, DV), lambda b, h, c=col // DV: (b, 0, c + h))
    in_specs = [
        qk(COL_GQ), qk(COL_GK), vv(COL_GV), vv(COL_GR),
        pl.BlockSpec((1, lp, V7X_LANES), lambda b, h: (b, 0, COL_GA // V7X_LANES)),
        qk(COL_RQ), qk(COL_RK), vv(COL_RV), vv(COL_RG),
        pl.BlockSpec((V7X_LANES, DK), lambda b, h: (0, h)),
        pl.BlockSpec((1, DK), lambda b, h: (0, h)),
        pl.BlockSpec((1, DV), lambda b, h: (0, 0)),
        pl.BlockSpec((1, DV), lambda b, h: (0, h)),
        pl.BlockSpec((1, DV), lambda b, h: (0, h)),
        pl.BlockSpec((lp, DK), lambda b, h: (0, 0)),
        pl.BlockSpec((lp, DK), lambda b, h: (0, 0)),
        pl.BlockSpec((1, CHUNK, CHUNK), lambda b, h: (h, 0, 0)),
        pl.BlockSpec((1, CHUNK, DK), lambda b, h: (h, 0, 0)),
        pl.BlockSpec((1, CHUNK, DK), lambda b, h: (h, 0, 0)),
        pl.BlockSpec((1, 1, DV), lambda b, h: (h, 0, 0)),
    ]
    out_spec = pl.BlockSpec((1, lp, DV), lambda b, h: (b, 0, h))
    return pl.pallas_call(
        functools.partial(_mixer_body, n_groups=n_chunks // gsz, gsz=gsz),
        grid=(bsz, HEADS),
        in_specs=in_specs,
        out_specs=[out_spec, out_spec],
        out_shape=[jax.ShapeDtypeStruct((bsz, lp, VW), BF16)] * 2,
        scratch_shapes=[pltpu.VMEM((DK, DV), F32), pltpu.VMEM((DK, DV), F32),
                        pltpu.VMEM((2, gsz * CHUNK, DK), F32),
                        pltpu.VMEM((2, gsz * CHUNK, DV), F32), pltpu.VMEM((2, gsz * CHUNK, DV), F32)],
        compiler_params=_params("arbitrary", "arbitrary"),
        name="mixer",
    )(*([proj3] * 4), proj3, *([proj3] * 4), wa2, ba, gng, rng, rnb, cos, sin, dmat, xi, zeta, rdec)


def _mix_out_body(ya_ref, yb_ref, gta_ref, gtb_ref, h_ref, wpa_ref, wpb_ref, wo_ref,
                  bga_ref, bgb_ref, g_ref, b_ref, o_ref, *, alpha):
    y_a = jnp.dot(ya_ref[...], wpa_ref[...], preferred_element_type=F32)
    y_b = jnp.dot(yb_ref[...], wpb_ref[...], preferred_element_type=F32)
    merged = (jax.nn.sigmoid(gta_ref[...].astype(F32) + bga_ref[...]) * y_a
              + jax.nn.sigmoid(gtb_ref[...].astype(F32) + bgb_ref[...]) * y_b)
    mix = jnp.dot(merged.astype(BF16), wo_ref[...], preferred_element_type=F32)
    o_ref[...] = _layer_norm(alpha * h_ref[...] + mix, g_ref[...], b_ref[...])


def _mix_out(ya, yb, proj, h, wpa, wpb, wo, bga, bgb, g, b, alpha):
    m, d = h.shape
    tm = _divisor_tile(m, 528, V7X_BF16_SUBLANES)
    rows = lambda: pl.BlockSpec((tm, d), lambda i: (i, 0))
    full = lambda r: pl.BlockSpec((r, d), lambda i: (0, 0))
    return pl.pallas_call(
        functools.partial(_mix_out_body, alpha=alpha),
        grid=(m // tm,),
        in_specs=[rows(), rows(),
                  pl.BlockSpec((tm, d), lambda i: (i, COL_GATE_A // d)),
                  pl.BlockSpec((tm, d), lambda i: (i, COL_GATE_B // d)),
                  rows(), full(d), full(d), full(d), full(1), full(1), full(1), full(1)],
        out_specs=rows(),
        out_shape=jax.ShapeDtypeStruct((m, d), F32),
        compiler_params=_params("arbitrary"),
        name="mix_out",
    )(ya, yb, proj, proj, h, wpa, wpb, wo, bga, bgb, g, b)


def _ffn_body(x_ref, wg_ref, wu_ref, wd_ref, g_ref, b_ref, o_ref, xb_ref, acc_ref, *, alpha):
    f = pl.program_id(1)

    @pl.when(f == 0)
    def _():
        xb_ref[...] = x_ref[...].astype(BF16)
        acc_ref[...] = jnp.zeros_like(acc_ref)

    xb = xb_ref[...]
    hid = (_silu(jnp.dot(xb, wg_ref[...], preferred_element_type=F32))
           * jnp.dot(xb, wu_ref[...], preferred_element_type=F32))
    acc_ref[...] += jnp.dot(hid.astype(BF16), wd_ref[...], preferred_element_type=F32)

    @pl.when(f == pl.num_programs(1) - 1)
    def _():
        o_ref[...] = _layer_norm(alpha * x_ref[...] + acc_ref[...], g_ref[...], b_ref[...])


def _ffn(h, wg, wu, wd, g, b, alpha):
    m, d = h.shape
    dff = wg.shape[1]
    tm = _divisor_tile(m, 528, V7X_BF16_SUBLANES)
    tf = _divisor_tile(dff, 1408, V7X_LANES)
    return pl.pallas_call(
        functools.partial(_ffn_body, alpha=alpha),
        grid=(m // tm, dff // tf),
        in_specs=[pl.BlockSpec((tm, d), lambda i, f: (i, 0)),
                  pl.BlockSpec((d, tf), lambda i, f: (0, f)),
                  pl.BlockSpec((d, tf), lambda i, f: (0, f)),
                  pl.BlockSpec((tf, d), lambda i, f: (f, 0)),
                  pl.BlockSpec((1, d), lambda i, f: (0, 0)),
                  pl.BlockSpec((1, d), lambda i, f: (0, 0))],
        out_specs=pl.BlockSpec((tm, d), lambda i, f: (i, 0)),
        out_shape=jax.ShapeDtypeStruct((m, d), F32),
        scratch_shapes=[pltpu.VMEM((tm, d), BF16), pltpu.VMEM((tm, d), F32)],
        compiler_params=_params("arbitrary", "arbitrary"),
        name="ffn",
    )(h, wg, wu, wd, g, b)


def _router_body(h_ref, wr_ref, idx_ref, wt_ref):
    logits = lax.dot_general(wr_ref[...], h_ref[...], (((1,), (1,)), ((), ())),
                             precision=lax.Precision.HIGHEST, preferred_element_type=F32)
    e = lax.broadcasted_iota(jnp.int32, logits.shape, 0)
    m1 = jnp.max(logits, axis=0, keepdims=True)
    i1 = jnp.min(jnp.where(logits == m1, e, N_EXPERTS), axis=0, keepdims=True)
    rest = jnp.where(e == i1, -jnp.inf, logits)
    m2 = jnp.max(rest, axis=0, keepdims=True)
    i2 = jnp.min(jnp.where(rest == m2, e, N_EXPERTS), axis=0, keepdims=True)
    e2 = jnp.exp(m2 - m1)
    w1 = 1.0 / (1.0 + e2)
    idx_ref[0:1, :] = i1
    idx_ref[1:2, :] = i2
    wt_ref[0:1, :] = w1
    wt_ref[1:2, :] = e2 * w1


def _router(h, wr_t):
    m, d = h.shape
    tm = _divisor_tile(m, 1024, V7X_LANES)
    return pl.pallas_call(
        _router_body,
        grid=(m // tm,),
        in_specs=[pl.BlockSpec((tm, d), lambda i: (i, 0)), pl.BlockSpec((N_EXPERTS, d), lambda i: (0, 0))],
        out_specs=[pl.BlockSpec((2, tm), lambda i: (0, i)), pl.BlockSpec((2, tm), lambda i: (0, i))],
        out_shape=[jax.ShapeDtypeStruct((2, m), jnp.int32), jax.ShapeDtypeStruct((2, m), F32)],
        compiler_params=_params("arbitrary"),
        name="router",
    )(h, wr_t)


def _index_prefetch(idx_hbm, idx_smem, isem, i, n):
    slot = lax.rem(i, 2)

    @pl.when(i == 0)
    def _():
        pltpu.make_async_copy(idx_hbm.at[0], idx_smem.at[0], isem.at[0]).start()

    pltpu.make_async_copy(idx_hbm.at[i], idx_smem.at[slot], isem.at[slot]).wait()

    @pl.when(i + 1 < n)
    def _():
        pltpu.make_async_copy(idx_hbm.at[i + 1], idx_smem.at[1 - slot], isem.at[1 - slot]).start()

    return slot


SLAB = 8
ZERO_ROWS = 128


def _slab_rows(p):
    return pl.ds(pl.multiple_of(p * SLAB, SLAB), SLAB)


def _slabs_to_matrix(ref2, rows):
    return jnp.concatenate([ref2[pl.ds(g, rows, stride=SLAB), :] for g in range(SLAB)], axis=1)


def _dispatch_body(st_ref, cnt_ref, pd_ref, pos_hbm, h3_ref, xs_hbm, idx_smem, zbuf, isem, sem, psem,
                   *, tm, n_rows):
    i = pl.program_id(0)
    n = pl.num_programs(0)
    slot = _index_prefetch(pos_hbm, idx_smem, isem, i, n)

    def issue(r, carry):
        p0 = idx_smem[slot, r]
        p1 = idx_smem[slot, tm + r]
        pltpu.make_async_copy(h3_ref.at[_slab_rows(r)], xs_hbm.at[_slab_rows(p0)], sem).start()
        pltpu.make_async_copy(h3_ref.at[_slab_rows(r)], xs_hbm.at[_slab_rows(p1)], sem).start(priority=1)
        return carry

    lax.fori_loop(0, tm, issue, 0, unroll=8)
    pltpu.make_async_copy(h3_ref, xs_hbm.at[pl.ds(0, tm * SLAB)], sem).wait()
    pltpu.make_async_copy(h3_ref, xs_hbm.at[pl.ds(0, tm * SLAB)], sem).wait()

    @pl.when(i == n - 1)
    def _():
        zbuf[...] = jnp.zeros_like(zbuf)
        zrow = zbuf.at[pl.ds(0, SLAB)]
        for e in range(N_EXPERTS):
            lo = st_ref[e] + cnt_ref[e]
            npad = pd_ref[e] - cnt_ref[e]

            def fill(r, carry):
                pltpu.make_async_copy(zrow, xs_hbm.at[_slab_rows(lo + r)], psem).start()
                return carry

            def drain(r, carry):
                pltpu.make_async_copy(zrow, xs_hbm.at[_slab_rows(lo + r)], psem).wait()
                return carry

            lax.fori_loop(0, npad, fill, 0)
            lax.fori_loop(0, npad, drain, 0)

        zr = zbuf.shape[0] // SLAB
        tail = st_ref[N_EXPERTS - 1] + pd_ref[N_EXPERTS - 1]
        nblk = (n_rows - tail) // zr

        def block(k):
            return xs_hbm.at[pl.ds(pl.multiple_of((tail + k * zr) * SLAB, SLAB), zr * SLAB)]

        def fill_block(k, carry):
            pltpu.make_async_copy(zbuf, block(k), psem).start()
            return carry

        def drain_block(k, carry):
            pltpu.make_async_copy(zbuf, block(k), psem).wait()
            return carry

        lax.fori_loop(0, nblk, fill_block, 0)
        lax.fori_loop(0, nblk, drain_block, 0)


def _dispatch(h3, pos2d, starts, counts, padded, n_rows, tm):
    m = h3.shape[0] // SLAB
    grid_spec = pltpu.PrefetchScalarGridSpec(
        num_scalar_prefetch=3,
        grid=(m // tm,),
        in_specs=[pl.BlockSpec(memory_space=pl.ANY),
                  pl.BlockSpec((tm * SLAB, V7X_LANES), lambda i, *_: (i, 0))],
        out_specs=pl.BlockSpec(memory_space=pl.ANY),
        scratch_shapes=[pltpu.SMEM((2, 2 * tm), jnp.int32), pltpu.VMEM((ZERO_ROWS * SLAB, V7X_LANES), F32),
                        pltpu.SemaphoreType.DMA((2,)), pltpu.SemaphoreType.DMA, pltpu.SemaphoreType.DMA],
    )
    return pl.pallas_call(
        functools.partial(_dispatch_body, tm=tm, n_rows=n_rows),
        grid_spec=grid_spec,
        out_shape=jax.ShapeDtypeStruct((n_rows * SLAB, V7X_LANES), F32),
        compiler_params=_params("arbitrary"),
        name="moe_dispatch",
    )(starts, counts, padded, pos2d, h3)


def _moe_body(te_ref, nu_ref, x_ref, wg_ref, wu_ref, wd_ref, o_ref, xb_ref, acc_ref):
    i = pl.program_id(0)
    f = pl.program_id(1)
    used = i < nu_ref[0]
    d = xb_ref.shape[1]

    @pl.when(f == 0)
    def _():
        acc_ref[...] = jnp.zeros_like(acc_ref)

    @pl.when(used & (f == 0))
    def _():
        xb_ref[...] = _slabs_to_matrix(x_ref, xb_ref.shape[0]).astype(BF16)

    @pl.when(used)
    def _():
        xb = xb_ref[...]
        hid = (_silu(jnp.dot(xb, wg_ref[0], preferred_element_type=F32))
               * jnp.dot(xb, wu_ref[0], preferred_element_type=F32))
        acc_ref[...] += jnp.dot(hid.astype(BF16), wd_ref[0], preferred_element_type=F32)

    @pl.when(f == pl.num_programs(1) - 1)
    def _():
        for g in range(SLAB):
            o_ref[pl.ds(g, acc_ref.shape[0], stride=SLAB), :] = acc_ref[:, g * V7X_LANES:(g + 1) * V7X_LANES]


def _moe_experts(xs, tile_expert, n_used, wg, wu, wd, tme):
    r = xs.shape[0] // SLAB
    d, dff = wg.shape[1], wg.shape[2]
    tf = _divisor_tile(dff, 896, V7X_LANES)
    grid_spec = pltpu.PrefetchScalarGridSpec(
        num_scalar_prefetch=2,
        grid=(r // tme, dff // tf),
        in_specs=[pl.BlockSpec((tme * SLAB, V7X_LANES), lambda i, f, te, nu: (i, 0)),
                  pl.BlockSpec((1, d, tf), lambda i, f, te, nu: (te[i], 0, f)),
                  pl.BlockSpec((1, d, tf), lambda i, f, te, nu: (te[i], 0, f)),
                  pl.BlockSpec((1, tf, d), lambda i, f, te, nu: (te[i], f, 0))],
        out_specs=pl.BlockSpec((tme * SLAB, V7X_LANES), lambda i, f, te, nu: (i, 0)),
        scratch_shapes=[pltpu.VMEM((tme, d), BF16), pltpu.VMEM((tme, d), F32)],
    )
    return pl.pallas_call(
        _moe_body,
        grid_spec=grid_spec,
        out_shape=jax.ShapeDtypeStruct(xs.shape, F32),
        compiler_params=_params("arbitrary", "arbitrary"),
        name="moe_experts",
    )(tile_expert, n_used, xs, wg, wu, wd)


def _combine_body(pos_hbm, ys_hbm, wt_ref, h_ref, g_ref, b_ref, o_ref, ybuf, idx_smem, isem, sem, *, tm, alpha):
    i = pl.program_id(0)
    slot = _index_prefetch(pos_hbm, idx_smem, isem, i, pl.num_programs(0))

    def issue(r, carry):
        p0 = idx_smem[slot, r]
        p1 = idx_smem[slot, tm + r]
        pltpu.make_async_copy(ys_hbm.at[_slab_rows(p0)], ybuf.at[0, _slab_rows(r)], sem).start()
        pltpu.make_async_copy(ys_hbm.at[_slab_rows(p1)], ybuf.at[1, _slab_rows(r)], sem).start(priority=1)
        return carry

    lax.fori_loop(0, tm, issue, 0, unroll=8)
    pltpu.make_async_copy(ys_hbm.at[pl.ds(0, tm * SLAB)], ybuf.at[0], sem).wait()
    pltpu.make_async_copy(ys_hbm.at[pl.ds(0, tm * SLAB)], ybuf.at[1], sem).wait()
    wt = wt_ref[...]
    out = wt[:, 0:1] * _slabs_to_matrix(ybuf.at[0], tm) + wt[:, 1:2] * _slabs_to_matrix(ybuf.at[1], tm)
    o_ref[...] = _layer_norm(alpha * h_ref[...] + out, g_ref[...], b_ref[...])


def _combine(pos2d, ys, wt, h, g, b, alpha, tm):
    m, d = h.shape
    return pl.pallas_call(
        functools.partial(_combine_body, tm=tm, alpha=alpha),
        grid=(m // tm,),
        in_specs=[pl.BlockSpec(memory_space=pl.ANY), pl.BlockSpec(memory_space=pl.ANY),
                  pl.BlockSpec((tm, 2), lambda i: (i, 0)),
                  pl.BlockSpec((tm, d), lambda i: (i, 0)),
                  pl.BlockSpec((1, d), lambda i: (0, 0)), pl.BlockSpec((1, d), lambda i: (0, 0))],
        out_specs=pl.BlockSpec((tm, d), lambda i: (i, 0)),
        out_shape=jax.ShapeDtypeStruct((m, d), F32),
        scratch_shapes=[pltpu.VMEM((2, tm * SLAB, V7X_LANES), F32), pltpu.SMEM((2, 2 * tm), jnp.int32),
                        pltpu.SemaphoreType.DMA((2,)), pltpu.SemaphoreType.DMA],
        compiler_params=_params("arbitrary"),
        name="moe_combine",
    )(pos2d, ys, wt, h, g, b)


def _moe_layer(h, wr, wg, wu, wd, g, b, alpha):
    m, d = h.shape
    assert d == SLAB * V7X_LANES
    tme = _divisor_tile(m, 1024, ZERO_ROWS)
    tmc = _divisor_tile(m, 512, 8)
    idx, wts = _router(h, wr.T)

    e_flat = idx.reshape(-1)
    onehot = (e_flat[:, None] == jnp.arange(N_EXPERTS, dtype=jnp.int32)[None, :]).astype(jnp.int32)
    csum = jnp.cumsum(onehot, axis=0)
    rank = jnp.take_along_axis(csum, e_flat[:, None], axis=1)[:, 0] - 1
    counts = csum[-1]
    padded = ((counts + tme - 1) // tme) * tme
    ends = jnp.cumsum(padded)
    starts = ends - padded
    pos = starts[e_flat] + rank
    n_rows = 2 * m + N_EXPERTS * tme
    n_tiles = n_rows // tme
    tile_start = jnp.arange(n_tiles, dtype=jnp.int32) * tme
    tile_expert = jnp.minimum(jnp.sum(tile_start[:, None] >= ends[None, :], axis=1), N_EXPERTS - 1).astype(jnp.int32)
    n_used = (ends[-1] // tme).astype(jnp.int32).reshape(1)
    pos2d = pos.reshape(2, m // tmc, tmc).transpose(1, 0, 2).reshape(m // tmc, 2 * tmc)

    xs = _dispatch(h.reshape(m * SLAB, V7X_LANES), pos2d, starts, counts, padded, n_rows, tmc)
    ys = _moe_experts(xs, tile_expert, n_used, wg, wu, wd, tme)
    return _combine(pos2d, ys, wts.T, h, g, b, alpha, tmc)


def kernel(x, meta, w_in, b_gate, gla_w_a2, gla_b_a, gla_norm_g, ret_norm_g, ret_norm_b, w_pa, w_pb, w_o,
           ln1_g, ln1_b, ffn_w_gate, ffn_w_up, ffn_w_down, moe_w_router, moe_w_gate, moe_w_up, moe_w_down,
           ln2_g, ln2_b):
    bsz, seq, d = x.shape
    depth = w_in.shape[0]
    alpha = (2 * depth) ** 0.25
    lp = seq + N_META + PAD
    assert lp % CHUNK == 0 and d == VW
    m = bsz * lp

    h = jnp.concatenate([jnp.zeros((bsz, PAD, d), x.dtype),
                         jnp.broadcast_to(meta[None].astype(x.dtype), (bsz, N_META, d)), x], axis=1)
    h = h.reshape(m, d)
    tables = _retention_tables(lp)
    keep = jnp.asarray(np.tile((np.arange(lp) >= PAD).astype(np.float32), bsz).reshape(m, 1))

    c0 = 2 * QK + 2 * VW
    row = lambda a: a.reshape(1, -1)
    for l in range(depth):
        w = w_in[l]
        w_perm = jnp.concatenate(
            [w[:, :c0], w[:, c0 + GLA_RANK:], w[:, c0:c0 + GLA_RANK],
             jnp.zeros((d, V7X_LANES - GLA_RANK), w.dtype)], axis=1).astype(BF16)
        wa2 = jnp.concatenate([gla_w_a2[l], jnp.zeros((V7X_LANES - GLA_RANK, QK), F32)], axis=0).astype(BF16)

        proj = _in_proj(h, keep, w_perm)
        ya, yb = _mixer(proj.reshape(bsz, lp, N_PROJ), wa2, row(gla_b_a[l]), row(gla_norm_g[l]),
                        row(ret_norm_g[l]), row(ret_norm_b[l]), tables)
        h = _mix_out(ya.reshape(m, VW), yb.reshape(m, VW), proj, h,
                     w_pa[l].astype(BF16), w_pb[l].astype(BF16), w_o[l].astype(BF16),
                     row(b_gate[l, 0]), row(b_gate[l, 1]), row(ln1_g[l]), row(ln1_b[l]), alpha)
        if l % 2 == 0:
            j = l // 2
            h = _ffn(h, ffn_w_gate[j].astype(BF16), ffn_w_up[j].astype(BF16), ffn_w_down[j].astype(BF16),
                     row(ln2_g[l]), row(ln2_b[l]), alpha)
        else:
            j = l // 2
            h = _moe_layer(h, moe_w_router[j], moe_w_gate[j].astype(BF16), moe_w_up[j].astype(BF16),
                           moe_w_down[j].astype(BF16), row(ln2_g[l]), row(ln2_b[l]), alpha)
    return h.reshape(bsz, lp, d)[:, PAD + N_META:]
```

```python
import functools
import math

import jax
import jax.numpy as jnp
import numpy as np
from jax import lax
from jax.experimental import pallas as pl
from jax.experimental.pallas import tpu as pltpu

F32 = jnp.float32
BF16 = jnp.bfloat16

N_META = 16
CHUNK = 64
PAD = CHUNK - N_META
HEADS = 4
DK = 128
DV = 256
GLA_RANK = 16
GLA_TAU = 16.0
ROPE_BASE = 10000.0
N_EXPERTS = 8
LN_EPS = 1e-5
GN_EPS = 1e-6

V7X_LANES = 128
V7X_BF16_SUBLANES = 16
V7X_MXU_COLS = 256
V7X_VMEM_LIMIT_BYTES = 56 * 1024 * 1024

SLAB = 8
ZERO_ROWS = 128

QK = HEADS * DK
VW = HEADS * DV
COL_GQ, COL_GK, COL_GV, COL_GR = 0, QK, 2 * QK, 2 * QK + VW
COL_RQ = 2 * QK + 2 * VW
COL_RK, COL_RV, COL_RG = COL_RQ + QK, COL_RQ + 2 * QK, COL_RQ + 2 * QK + VW
COL_GATE_A = COL_RQ + 2 * QK + 2 * VW
COL_GATE_B = COL_GATE_A + QK + QK
COL_GA = COL_GATE_B + QK + QK
N_PROJ = COL_GA + V7X_MXU_COLS


def _divisor_tile(n, cap, multiple):
    best = None
    for t in range(multiple, min(n, cap) + 1, multiple):
        if n % t == 0:
            best = t
    assert best is not None, (n, cap, multiple)
    return best


def _params(*sem):
    return pltpu.CompilerParams(dimension_semantics=sem, vmem_limit_bytes=V7X_VMEM_LIMIT_BYTES)


def _resident(shape):
    return pl.BlockSpec(shape, lambda *_: (0,) * len(shape), pipeline_mode=pl.Buffered(1))


def _layer_norm(r, g, b):
    mu = jnp.mean(r, axis=-1, keepdims=True)
    d = r - mu
    var = jnp.mean(d * d, axis=-1, keepdims=True)
    return d * lax.rsqrt(var + LN_EPS) * g + b


def _silu(x):
    return x * jax.nn.sigmoid(x)


def _in_proj_body(x_ref, keep_ref, w_ref, o_ref, xb_ref):
    @pl.when(pl.program_id(1) == 0)
    def _():
        xb_ref[...] = jnp.where(keep_ref[...] > 0.0, x_ref[...], 0.0).astype(BF16)

    xb = xb_ref[...]
    for c0 in range(0, o_ref.shape[1], V7X_MXU_COLS):
        cols = slice(c0, c0 + V7X_MXU_COLS)
        o_ref[:, cols] = jnp.dot(xb, w_ref[:, cols], preferred_element_type=F32).astype(BF16)


def _in_proj(h, keep, w):
    m, d = h.shape
    n = w.shape[1]
    tm = _divisor_tile(m, 1056, V7X_BF16_SUBLANES)
    tn = _divisor_tile(n, 2816, V7X_MXU_COLS)
    return pl.pallas_call(
        _in_proj_body,
        grid=(m // tm, n // tn),
        in_specs=[pl.BlockSpec((tm, d), lambda i, j: (i, 0)), pl.BlockSpec((tm, 1), lambda i, j: (i, 0)),
                  pl.BlockSpec((d, tn), lambda i, j: (0, j))],
        out_specs=pl.BlockSpec((tm, tn), lambda i, j: (i, j)),
        out_shape=jax.ShapeDtypeStruct((m, n), BF16),
        scratch_shapes=[pltpu.VMEM((tm, d), BF16)],
        compiler_params=_params("arbitrary", "arbitrary"),
        name="in_proj",
    )(h, keep, w)


def _mixer_body(gq_ref, gk_ref, gv_ref, gr_ref, ga_ref, rq_ref, rk_ref, rv_ref, rg_ref,
                wa2_ref, ba_ref, gng_ref, rng_ref, rnb_ref, cos_ref, sin_ref,
                dmat_ref, xi_ref, zeta_ref, rdec_ref,
                ya_ref, yb_ref, sa_ref, sb_ref, b_ref, oa_ref, ob_ref, *, n_groups, gsz):
    c_ = CHUNK
    r_ = gsz * c_
    sa_ref[...] = jnp.zeros_like(sa_ref)
    sb_ref[...] = jnp.zeros_like(sb_ref)
    row = lax.broadcasted_iota(jnp.int32, (c_, c_), 0)
    col = lax.broadcasted_iota(jnp.int32, (c_, c_), 1)
    causal = row >= col
    grow = lax.broadcasted_iota(jnp.int32, (r_, r_), 0)
    gcol = lax.broadcasted_iota(jnp.int32, (r_, r_), 1)
    shift = int(math.log2(c_))
    gtri = jnp.where((jnp.right_shift(grow, shift) == jnp.right_shift(gcol, shift)) & (grow >= gcol),
                     1.0, 0.0).astype(BF16)
    wa2 = wa2_ref[...]
    ba = ba_ref[...]
    gng = gng_ref[...]
    rng = rng_ref[...]
    rnb = rnb_ref[...]
    dmat = dmat_ref[0]
    xi = xi_ref[0]
    zeta = zeta_ref[0]
    rdec = rdec_ref[0]
    nt = (((1,), (1,)), ((), ()))
    tn = (((0,), (0,)), ((), ()))
    mm = functools.partial(jnp.dot, preferred_element_type=F32)
    mmg = functools.partial(lax.dot_general, preferred_element_type=F32)
    chunks = [slice(g * c_, (g + 1) * c_) for g in range(gsz)]

    def group_rows(j):
        return pl.ds(pl.multiple_of(j * r_, c_), r_)

    def gate_logits(j):
        return mm(ga_ref[0, group_rows(j), :], wa2) + ba

    def log_decay(z, j):
        valid = j * r_ + lax.broadcasted_iota(jnp.int32, (r_, 1), 0) >= PAD
        log_sig = jnp.minimum(z, 0.0) - jnp.log1p(jnp.exp(-jnp.abs(z)))
        la = jnp.where(valid, log_sig * (1.0 / GLA_TAU), 0.0)
        la_hi = la.astype(BF16)
        la_lo = (la - la_hi.astype(F32)).astype(BF16)
        return mm(gtri, la_hi) + mm(gtri, la_lo)

    def finish(j, slot):
        rows = group_rows(j)
        o = oa_ref[slot]
        ms = jnp.mean(o * o, axis=-1, keepdims=True)
        on = o * lax.rsqrt(ms + LN_EPS) * gng
        ya_ref[0, rows, :] = (_silu(gr_ref[0, rows, :].astype(F32)) * on).astype(BF16)
        ob = ob_ref[slot]
        mu = jnp.mean(ob, axis=-1, keepdims=True)
        dd = ob - mu
        var = jnp.mean(dd * dd, axis=-1, keepdims=True)
        obn = dd * lax.rsqrt(var + GN_EPS) * rng + rnb
        yb_ref[0, rows, :] = (_silu(rg_ref[0, rows, :].astype(F32)) * obn).astype(BF16)

    oa_ref[...] = jnp.zeros_like(oa_ref)
    ob_ref[...] = jnp.zeros_like(ob_ref)
    b_ref[0] = log_decay(gate_logits(0), 0)

    def group(gi, carry):
        slot = lax.rem(gi, 2)
        nxt = jnp.minimum(gi + 1, n_groups - 1)
        z_next = gate_logits(nxt)

        rows = group_rows(gi)
        cos = cos_ref[rows, :]
        sin = sin_ref[rows, :]
        rq = rq_ref[0, rows, :].astype(F32)
        rk = rk_ref[0, rows, :].astype(F32)
        qr = rq * cos + pltpu.roll(rq, DK // 2, 1) * sin
        kr = (rk * cos + pltpu.roll(rk, DK // 2, 1) * sin) * (DK ** -0.5)
        vb = rv_ref[0, rows, :]
        qr_b = qr.astype(BF16)
        kr_b = kr.astype(BF16)
        s_b = [mmg(qr_b[cs], kr_b[cs], nt) for cs in chunks]
        u_b = [mmg((kr[cs] * zeta).astype(BF16), vb[cs], tn) for cs in chunks]

        b = b_ref[slot]
        q = gq_ref[0, rows, :].astype(F32) * (DK ** -0.5)
        k = gk_ref[0, rows, :].astype(F32)
        v = gv_ref[0, rows, :]
        q_in = (q * jnp.exp(b)).astype(BF16)
        k_in = (k * jnp.exp(-b)).astype(BF16)
        s_a = [mmg(q_in[cs], k_in[cs], nt) for cs in chunks]
        u_a, dfull = [], []
        for cs in chunks:
            b_c = b[cs]
            b_last = b_c[c_ - 1:c_, :]
            k_tail = (k[cs] * jnp.exp(b_last - b_c)).astype(BF16)
            u_a.append(mmg(k_tail, v[cs], tn))
            dcol = jnp.transpose(jnp.broadcast_to(jnp.exp(b_last), (DK, DK)))
            dfull.append(jnp.concatenate([dcol, dcol], axis=1))

        b_ref[1 - slot] = log_decay(z_next, nxt)
        finish(jnp.maximum(gi - 1, 0), 1 - slot)

        st_a = [sa_ref[...]]
        st_b = [sb_ref[...]]
        for g in range(gsz):
            st_a.append(st_a[g] * dfull[g] + u_a[g])
            st_b.append(st_b[g] * rdec + u_b[g])
        sa_ref[...] = st_a[gsz]
        sb_ref[...] = st_b[gsz]

        o_a, o_b = [], []
        for g, cs in enumerate(chunks):
            o_b.append(mm((s_b[g] * dmat).astype(BF16), vb[cs])
                       + mm((qr[cs] * xi).astype(BF16), st_b[g].astype(BF16)))
        for g, cs in enumerate(chunks):
            sm = jnp.where(causal, s_a[g], 0.0).astype(BF16)
            o_a.append(mm(sm, v[cs]) + mm(q_in[cs], st_a[g].astype(BF16)))
        oa_ref[slot] = jnp.concatenate(o_a, axis=0)
        ob_ref[slot] = jnp.concatenate(o_b, axis=0)
        return carry

    lax.fori_loop(0, n_groups, group, 0)
    finish(n_groups - 1, (n_groups - 1) % 2)


def _retention_tables(lp):
    half = DK // 2
    pos = np.arange(lp, dtype=np.float64) - PAD
    inv = ROPE_BASE ** (-np.arange(half, dtype=np.float64) / half)
    ang = pos[:, None] * inv[None, :]
    cos = np.concatenate([np.cos(ang), np.cos(ang)], axis=1)
    sin = np.concatenate([-np.sin(ang), np.sin(ang)], axis=1)
    lg = np.log(1.0 - 2.0 ** (-5.0 - np.arange(HEADS, dtype=np.float64)))
    idx = np.arange(CHUNK, dtype=np.float64)
    rel = idx[:, None] - idx[None, :]
    dmat = np.where(rel >= 0, np.exp(lg[:, None, None] * np.maximum(rel, 0.0)), 0.0)
    xi = np.broadcast_to(np.exp(lg[:, None] * (idx + 1.0))[:, :, None], (HEADS, CHUNK, DK))
    zeta = np.broadcast_to(np.exp(lg[:, None] * (CHUNK - 1.0 - idx))[:, :, None], (HEADS, CHUNK, DK))
    rdec = np.broadcast_to(np.exp(lg * CHUNK)[:, None, None], (HEADS, 1, DV))
    f = lambda a: jnp.asarray(np.ascontiguousarray(a), F32)
    return f(cos), f(sin), f(dmat), f(xi), f(zeta), f(rdec)


def _mixer(proj3, wa2, ba, gng, rng, rnb, tables):
    bsz, lp, _ = proj3.shape
    cos, sin, dmat, xi, zeta, rdec = tables
    n_chunks = lp // CHUNK
    gsz = 3 if n_chunks % 3 == 0 else 1
    qk = lambda col: pl.BlockSpec((1, lp, DK), lambda b, h, c=col // DK: (b, 0, c + h))
    vv = lambda col: pl.BlockSpec((1, lp, DV), lambda b, h, c=col // DV: (b, 0, c + h))
    in_specs = [
        qk(COL_GQ), qk(COL_GK), vv(COL_GV), vv(COL_GR),
        pl.BlockSpec((1, lp, V7X_LANES), lambda b, h: (b, 0, COL_GA // V7X_LANES)),
        qk(COL_RQ), qk(COL_RK), vv(COL_RV), vv(COL_RG),
        pl.BlockSpec((V7X_LANES, DK), lambda b, h: (0, h)),
        pl.BlockSpec((1, DK), lambda b, h: (0, h)),
        pl.BlockSpec((1, DV), lambda b, h: (0, 0)),
        pl.BlockSpec((1, DV), lambda b, h: (0, h)),
        pl.BlockSpec((1, DV), lambda b, h: (0, h)),
        pl.BlockSpec((lp, DK), lambda b, h: (0, 0)),
        pl.BlockSpec((lp, DK), lambda b, h: (0, 0)),
        pl.BlockSpec((1, CHUNK, CHUNK), lambda b, h: (h, 0, 0)),
        pl.BlockSpec((1, CHUNK, DK), lambda b, h: (h, 0, 0)),
        pl.BlockSpec((1, CHUNK, DK), lambda b, h: (h, 0, 0)),
        pl.BlockSpec((1, 1, DV), lambda b, h: (h, 0, 0)),
    ]
    out_spec = pl.BlockSpec((1, lp, DV), lambda b, h: (b, 0, h))
    return pl.pallas_call(
        functools.partial(_mixer_body, n_groups=n_chunks // gsz, gsz=gsz),
        grid=(bsz, HEADS),
        in_specs=in_specs,
        out_specs=[out_spec, out_spec],
        out_shape=[jax.ShapeDtypeStruct((bsz, lp, VW), BF16)] * 2,
        scratch_shapes=[pltpu.VMEM((DK, DV), F32), pltpu.VMEM((DK, DV), F32),
                        pltpu.VMEM((2, gsz * CHUNK, DK), F32),
                        pltpu.VMEM((2, gsz * CHUNK, DV), F32), pltpu.VMEM((2, gsz * CHUNK, DV), F32)],
        compiler_params=_params("arbitrary", "arbitrary"),
        name="mixer",
    )(*([proj3] * 4), proj3, *([proj3] * 4), wa2, ba, gng, rng, rnb, cos, sin, dmat, xi, zeta, rdec)


def _mix_out_body(ya_ref, yb_ref, gta_ref, gtb_ref, h_ref, wpa_ref, wpb_ref, wo_ref,
                  bga_ref, bgb_ref, g_ref, b_ref, o_ref, *rest, alpha, with_slabs):
    mg_ref = rest[-1]
    ya = ya_ref[...]
    yb = yb_ref[...]
    for c0 in range(0, mg_ref.shape[1], V7X_MXU_COLS):
        cols = slice(c0, c0 + V7X_MXU_COLS)
        y_a = jnp.dot(ya, wpa_ref[:, cols], preferred_element_type=F32)
        y_b = jnp.dot(yb, wpb_ref[:, cols], preferred_element_type=F32)
        mg_ref[:, cols] = (jax.nn.sigmoid(gta_ref[:, cols].astype(F32) + bga_ref[:, cols]) * y_a
                           + jax.nn.sigmoid(gtb_ref[:, cols].astype(F32) + bgb_ref[:, cols]) * y_b).astype(BF16)
    mix = jnp.dot(mg_ref[...], wo_ref[...], preferred_element_type=F32)
    out = _layer_norm(alpha * h_ref[...] + mix, g_ref[...], b_ref[...])
    o_ref[...] = out
    if with_slabs:
        hs_ref = rest[0]
        for g in range(SLAB):
            hs_ref[pl.ds(g, out.shape[0], stride=SLAB), :] = out[:, g * V7X_LANES:(g + 1) * V7X_LANES]


def _mix_out(ya, yb, proj, h, wpa, wpb, wo, bga, bgb, g, b, alpha, with_slabs):
    m, d = h.shape
    tm = _divisor_tile(m, 1056, V7X_BF16_SUBLANES)
    rows = lambda: pl.BlockSpec((tm, d), lambda i: (i, 0))
    out_specs = [rows()]
    out_shape = [jax.ShapeDtypeStruct((m, d), F32)]
    if with_slabs:
        assert d == SLAB * V7X_LANES
        out_specs.append(pl.BlockSpec((tm * SLAB, V7X_LANES), lambda i: (i, 0)))
        out_shape.append(jax.ShapeDtypeStruct((m * SLAB, V7X_LANES), F32))
    return pl.pallas_call(
        functools.partial(_mix_out_body, alpha=alpha, with_slabs=with_slabs),
        grid=(m // tm,),
        in_specs=[rows(), rows(),
                  pl.BlockSpec((tm, d), lambda i: (i, COL_GATE_A // d)),
                  pl.BlockSpec((tm, d), lambda i: (i, COL_GATE_B // d)),
                  rows(), _resident((d, d)), _resident((d, d)), _resident((d, d)),
                  _resident((1, d)), _resident((1, d)), _resident((1, d)), _resident((1, d))],
        out_specs=out_specs,
        out_shape=out_shape,
        scratch_shapes=[pltpu.VMEM((tm, d), BF16)],
        compiler_params=_params("arbitrary"),
        name="mix_out",
    )(ya, yb, proj, proj, h, wpa, wpb, wo, bga, bgb, g, b)


def _swiglu_hidden(xb, wg_ref, wu_ref, hid_ref, widx=()):
    dff = hid_ref.shape[1]
    for c0 in range(0, dff, V7X_MXU_COLS):
        cols = slice(c0, c0 + V7X_MXU_COLS)
        gate = jnp.dot(xb, wg_ref[widx + (slice(None), cols)], preferred_element_type=F32)
        up = jnp.dot(xb, wu_ref[widx + (slice(None), cols)], preferred_element_type=F32)
        hid_ref[:, cols] = (_silu(gate) * up).astype(BF16)


def _ffn_body(x_ref, wg_ref, wu_ref, wd_ref, g_ref, b_ref, o_ref, hid_ref, *, alpha):
    x = x_ref[...]
    _swiglu_hidden(x.astype(BF16), wg_ref, wu_ref, hid_ref)
    f = jnp.dot(hid_ref[...], wd_ref[...], preferred_element_type=F32)
    o_ref[...] = _layer_norm(alpha * x + f, g_ref[...], b_ref[...])


def _ffn(h, wg, wu, wd, g, b, alpha):
    m, d = h.shape
    dff = wg.shape[1]
    assert dff % V7X_MXU_COLS == 0
    tm = _divisor_tile(m, 1056, V7X_BF16_SUBLANES)
    return pl.pallas_call(
        functools.partial(_ffn_body, alpha=alpha),
        grid=(m // tm,),
        in_specs=[pl.BlockSpec((tm, d), lambda i: (i, 0)),
                  _resident((d, dff)), _resident((d, dff)), _resident((dff, d)),
                  _resident((1, d)), _resident((1, d))],
        out_specs=pl.BlockSpec((tm, d), lambda i: (i, 0)),
        out_shape=jax.ShapeDtypeStruct((m, d), F32),
        scratch_shapes=[pltpu.VMEM((tm, dff), BF16)],
        compiler_params=_params("arbitrary"),
        name="ffn",
    )(h, wg, wu, wd, g, b)


def _router_body(h_ref, wr_ref, idx_ref, wt_ref):
    logits = lax.dot_general(wr_ref[...], h_ref[...], (((1,), (1,)), ((), ())),
                             precision=lax.Precision.HIGHEST, preferred_element_type=F32)
    e = lax.broadcasted_iota(jnp.int32, logits.shape, 0)
    m1 = jnp.max(logits, axis=0, keepdims=True)
    i1 = jnp.min(jnp.where(logits == m1, e, N_EXPERTS), axis=0, keepdims=True)
    rest = jnp.where(e == i1, -jnp.inf, logits)
    m2 = jnp.max(rest, axis=0, keepdims=True)
    i2 = jnp.min(jnp.where(rest == m2, e, N_EXPERTS), axis=0, keepdims=True)
    e2 = jnp.exp(m2 - m1)
    w1 = 1.0 / (1.0 + e2)
    idx_ref[0:1, :] = i1
    idx_ref[1:2, :] = i2
    wt_ref[0:1, :] = w1
    wt_ref[1:2, :] = e2 * w1


def _router(h, wr_t):
    m, d = h.shape
    tm = _divisor_tile(m, 1024, V7X_LANES)
    return pl.pallas_call(
        _router_body,
        grid=(m // tm,),
        in_specs=[pl.BlockSpec((tm, d), lambda i: (i, 0)), pl.BlockSpec((N_EXPERTS, d), lambda i: (0, 0))],
        out_specs=[pl.BlockSpec((2, tm), lambda i: (0, i)), pl.BlockSpec((2, tm), lambda i: (0, i))],
        out_shape=[jax.ShapeDtypeStruct((2, m), jnp.int32), jax.ShapeDtypeStruct((2, m), F32)],
        compiler_params=_params("arbitrary"),
        name="router",
    )(h, wr_t)


def _index_prefetch(idx_hbm, idx_smem, isem, i, n):
    slot = lax.rem(i, 2)

    @pl.when(i == 0)
    def _():
        pltpu.make_async_copy(idx_hbm.at[0], idx_smem.at[0], isem.at[0]).start()

    pltpu.make_async_copy(idx_hbm.at[i], idx_smem.at[slot], isem.at[slot]).wait()

    @pl.when(i + 1 < n)
    def _():
        pltpu.make_async_copy(idx_hbm.at[i + 1], idx_smem.at[1 - slot], isem.at[1 - slot]).start()

    return slot


def _slab_rows(p):
    return pl.ds(pl.multiple_of(p * SLAB, SLAB), SLAB)


def _slabs_to_matrix(ref2, rows):
    return jnp.concatenate([ref2[pl.ds(g, rows, stride=SLAB), :] for g in range(SLAB)], axis=1)


def _dispatch_body(st_ref, cnt_ref, pd_ref, pos_hbm, h3_ref, xs_hbm, idx_smem, zbuf, isem, sem, psem,
                   *, tm, n_rows):
    i = pl.program_id(0)
    n = pl.num_programs(0)
    slot = _index_prefetch(pos_hbm, idx_smem, isem, i, n)

    def issue(r, carry):
        p0 = idx_smem[slot, r]
        p1 = idx_smem[slot, tm + r]
        pltpu.make_async_copy(h3_ref.at[_slab_rows(r)], xs_hbm.at[_slab_rows(p0)], sem).start()
        pltpu.make_async_copy(h3_ref.at[_slab_rows(r)], xs_hbm.at[_slab_rows(p1)], sem).start(priority=1)
        return carry

    lax.fori_loop(0, tm, issue, 0, unroll=8)
    pltpu.make_async_copy(h3_ref, xs_hbm.at[pl.ds(0, tm * SLAB)], sem).wait()
    pltpu.make_async_copy(h3_ref, xs_hbm.at[pl.ds(0, tm * SLAB)], sem).wait()

    @pl.when(i == n - 1)
    def _():
        zbuf[...] = jnp.zeros_like(zbuf)
        zrow = zbuf.at[pl.ds(0, SLAB)]
        for e in range(N_EXPERTS):
            lo = st_ref[e] + cnt_ref[e]
            npad = pd_ref[e] - cnt_ref[e]

            def fill(r, carry):
                pltpu.make_async_copy(zrow, xs_hbm.at[_slab_rows(lo + r)], psem).start()
                return carry

            def drain(r, carry):
                pltpu.make_async_copy(zrow, xs_hbm.at[_slab_rows(lo + r)], psem).wait()
                return carry

            lax.fori_loop(0, npad, fill, 0)
            lax.fori_loop(0, npad, drain, 0)

        zr = zbuf.shape[0] // SLAB
        tail = st_ref[N_EXPERTS - 1] + pd_ref[N_EXPERTS - 1]
        nblk = (n_rows - tail) // zr

        def block(k):
            return xs_hbm.at[pl.ds(pl.multiple_of((tail + k * zr) * SLAB, SLAB), zr * SLAB)]

        def fill_block(k, carry):
            pltpu.make_async_copy(zbuf, block(k), psem).start()
            return carry

        def drain_block(k, carry):
            pltpu.make_async_copy(zbuf, block(k), psem).wait()
            return carry

        lax.fori_loop(0, nblk, fill_block, 0)
        lax.fori_loop(0, nblk, drain_block, 0)


def _dispatch(h3, pos2d, starts, counts, padded, n_rows, tm):
    m = h3.shape[0] // SLAB
    grid_spec = pltpu.PrefetchScalarGridSpec(
        num_scalar_prefetch=3,
        grid=(m // tm,),
        in_specs=[pl.BlockSpec(memory_space=pl.ANY),
                  pl.BlockSpec((tm * SLAB, V7X_LANES), lambda i, *_: (i, 0))],
        out_specs=pl.BlockSpec(memory_space=pl.ANY),
        scratch_shapes=[pltpu.SMEM((2, 2 * tm), jnp.int32), pltpu.VMEM((ZERO_ROWS * SLAB, V7X_LANES), F32),
                        pltpu.SemaphoreType.DMA((2,)), pltpu.SemaphoreType.DMA, pltpu.SemaphoreType.DMA],
    )
    return pl.pallas_call(
        functools.partial(_dispatch_body, tm=tm, n_rows=n_rows),
        grid_spec=grid_spec,
        out_shape=jax.ShapeDtypeStruct((n_rows * SLAB, V7X_LANES), F32),
        compiler_params=_params("arbitrary"),
        name="moe_dispatch",
    )(starts, counts, padded, pos2d, h3)


def _moe_body(te_ref, nu_ref, x_ref, wg_ref, wu_ref, wd_ref, o_ref, hid_ref):
    rows = hid_ref.shape[0]

    @pl.when(pl.program_id(0) < nu_ref[0])
    def _():
        xb = _slabs_to_matrix(x_ref, rows).astype(BF16)
        _swiglu_hidden(xb, wg_ref, wu_ref, hid_ref, widx=(0,))
        y = jnp.dot(hid_ref[...], wd_ref[0], preferred_element_type=F32)
        for g in range(SLAB):
            o_ref[pl.ds(g, rows, stride=SLAB), :] = y[:, g * V7X_LANES:(g + 1) * V7X_LANES]

    @pl.when(pl.program_id(0) >= nu_ref[0])
    def _():
        o_ref[...] = jnp.zeros_like(o_ref)


def _moe_experts(xs, tile_expert, n_used, wg, wu, wd, tme):
    r = xs.shape[0] // SLAB
    d, dff = wg.shape[1], wg.shape[2]
    assert dff % V7X_MXU_COLS == 0
    expert = lambda shape: pl.BlockSpec((1,) + shape, lambda i, te, nu: (te[i], 0, 0), pipeline_mode=pl.Buffered(1))
    grid_spec = pltpu.PrefetchScalarGridSpec(
        num_scalar_prefetch=2,
        grid=(r // tme,),
        in_specs=[pl.BlockSpec((tme * SLAB, V7X_LANES), lambda i, te, nu: (i, 0)),
                  expert((d, dff)), expert((d, dff)), expert((dff, d))],
        out_specs=pl.BlockSpec((tme * SLAB, V7X_LANES), lambda i, te, nu: (i, 0)),
        scratch_shapes=[pltpu.VMEM((tme, dff), BF16)],
    )
    return pl.pallas_call(
        _moe_body,
        grid_spec=grid_spec,
        out_shape=jax.ShapeDtypeStruct(xs.shape, F32),
        compiler_params=_params("arbitrary"),
        name="moe_experts",
    )(tile_expert, n_used, xs, wg, wu, wd)


def _combine_body(pos_hbm, ys_hbm, wt_ref, h_ref, g_ref, b_ref, o_ref, ybuf, idx_smem, isem, sem, *, tm, alpha):
    i = pl.program_id(0)
    slot = _index_prefetch(pos_hbm, idx_smem, isem, i, pl.num_programs(0))

    def issue(r, carry):
        p0 = idx_smem[slot, r]
        p1 = idx_smem[slot, tm + r]
        pltpu.make_async_copy(ys_hbm.at[_slab_rows(p0)], ybuf.at[0, _slab_rows(r)], sem).start()
        pltpu.make_async_copy(ys_hbm.at[_slab_rows(p1)], ybuf.at[1, _slab_rows(r)], sem).start(priority=1)
        return carry

    lax.fori_loop(0, tm, issue, 0, unroll=8)
    pltpu.make_async_copy(ys_hbm.at[pl.ds(0, tm * SLAB)], ybuf.at[0], sem).wait()
    pltpu.make_async_copy(ys_hbm.at[pl.ds(0, tm * SLAB)], ybuf.at[1], sem).wait()
    wt = wt_ref[...]
    out = wt[:, 0:1] * _slabs_to_matrix(ybuf.at[0], tm) + wt[:, 1:2] * _slabs_to_matrix(ybuf.at[1], tm)
    o_ref[...] = _layer_norm(alpha * h_ref[...] + out, g_ref[...], b_ref[...])


def _combine(pos2d, ys, wt, h, g, b, alpha, tm):
    m, d = h.shape
    return pl.pallas_call(
        functools.partial(_combine_body, tm=tm, alpha=alpha),
        grid=(m // tm,),
        in_specs=[pl.BlockSpec(memory_space=pl.ANY), pl.BlockSpec(memory_space=pl.ANY),
                  pl.BlockSpec((tm, 2), lambda i: (i, 0)),
                  pl.BlockSpec((tm, d), lambda i: (i, 0)),
                  pl.BlockSpec((1, d), lambda i: (0, 0)), pl.BlockSpec((1, d), lambda i: (0, 0))],
        out_specs=pl.BlockSpec((tm, d), lambda i: (i, 0)),
        out_shape=jax.ShapeDtypeStruct((m, d), F32),
        scratch_shapes=[pltpu.VMEM((2, tm * SLAB, V7X_LANES), F32), pltpu.SMEM((2, 2 * tm), jnp.int32),
                        pltpu.SemaphoreType.DMA((2,)), pltpu.SemaphoreType.DMA],
        compiler_params=_params("arbitrary"),
        name="moe_combine",
    )(pos2d, ys, wt, h, g, b)


def _moe_layer(h, h_slabs, wr, wg, wu, wd, g, b, alpha):
    m, d = h.shape
    assert d == SLAB * V7X_LANES
    tme = _divisor_tile(m, 1024, ZERO_ROWS)
    tmc = _divisor_tile(m, 512, 8)
    idx, wts = _router(h, wr.T)

    e_flat = idx.reshape(-1)
    onehot = (e_flat[:, None] == jnp.arange(N_EXPERTS, dtype=jnp.int32)[None, :]).astype(jnp.int32)
    csum = jnp.cumsum(onehot, axis=0)
    rank = jnp.take_along_axis(csum, e_flat[:, None], axis=1)[:, 0] - 1
    counts = csum[-1]
    padded = ((counts + tme - 1) // tme) * tme
    ends = jnp.cumsum(padded)
    starts = ends - padded
    pos = starts[e_flat] + rank
    n_rows = 2 * m + N_EXPERTS * tme
    n_tiles = n_rows // tme
    tile_start = jnp.arange(n_tiles, dtype=jnp.int32) * tme
    tile_expert = jnp.minimum(jnp.sum(tile_start[:, None] >= ends[None, :], axis=1), N_EXPERTS - 1).astype(jnp.int32)
    n_used = (ends[-1] // tme).astype(jnp.int32).reshape(1)
    pos2d = pos.reshape(2, m // tmc, tmc).transpose(1, 0, 2).reshape(m // tmc, 2 * tmc)

    xs = _dispatch(h_slabs, pos2d, starts, counts, padded, n_rows, tmc)
    ys = _moe_experts(xs, tile_expert, n_used, wg, wu, wd, tme)
    return _combine(pos2d, ys, wts.T, h, g, b, alpha, tmc)


def kernel(x, meta, w_in, b_gate, gla_w_a2, gla_b_a, gla_norm_g, ret_norm_g, ret_norm_b, w_pa, w_pb, w_o,
           ln1_g, ln1_b, ffn_w_gate, ffn_w_up, ffn_w_down, moe_w_router, moe_w_gate, moe_w_up, moe_w_down,
           ln2_g, ln2_b):
    bsz, seq, d = x.shape
    depth = w_in.shape[0]
    alpha = (2 * depth) ** 0.25
    lp = seq + N_META + PAD
    assert lp % CHUNK == 0 and d == VW
    m = bsz * lp

    h = jnp.concatenate([jnp.zeros((bsz, PAD, d), x.dtype),
                         jnp.broadcast_to(meta[None].astype(x.dtype), (bsz, N_META, d)), x], axis=1)
    h = h.reshape(m, d)
    tables = _retention_tables(lp)
    keep = jnp.asarray(np.tile((np.arange(lp) >= PAD).astype(np.float32), bsz).reshape(m, 1))

    c0 = 2 * QK + 2 * VW
    row = lambda a: a.reshape(1, -1)
    for l in range(depth):
        w = w_in[l]
        w_perm = jnp.concatenate(
            [w[:, :c0], w[:, c0 + GLA_RANK:], w[:, c0:c0 + GLA_RANK],
             jnp.zeros((d, V7X_MXU_COLS - GLA_RANK), w.dtype)], axis=1).astype(BF16)
        wa2 = jnp.concatenate([gla_w_a2[l], jnp.zeros((V7X_LANES - GLA_RANK, QK), F32)], axis=0).astype(BF16)

        proj = _in_proj(h, keep, w_perm)
        ya, yb = _mixer(proj.reshape(bsz, lp, N_PROJ), wa2, row(gla_b_a[l]), row(gla_norm_g[l]),
                        row(ret_norm_g[l]), row(ret_norm_b[l]), tables)
        is_moe = l % 2 == 1
        mixed = _mix_out(ya.reshape(m, VW), yb.reshape(m, VW), proj, h,
                         w_pa[l].astype(BF16), w_pb[l].astype(BF16), w_o[l].astype(BF16),
                         row(b_gate[l, 0]), row(b_gate[l, 1]), row(ln1_g[l]), row(ln1_b[l]), alpha, is_moe)
        h = mixed[0]
        if l % 2 == 0:
            j = l // 2
            h = _ffn(h, ffn_w_gate[j].astype(BF16), ffn_w_up[j].astype(BF16), ffn_w_down[j].astype(BF16),
                     row(ln2_g[l]), row(ln2_b[l]), alpha)
        else:
            j = l // 2
            h = _moe_layer(h, mixed[1], moe_w_router[j], moe_w_gate[j].astype(BF16), moe_w_up[j].astype(BF16),
                           moe_w_down[j].astype(BF16), row(ln2_g[l]), row(ln2_b[l]), alpha)
    return h.reshape(bsz, lp, d)[:, PAD + N_META:]
```

```python
import functools
import math

import jax
import jax.numpy as jnp
import numpy as np
from jax import lax
from jax.experimental import pallas as pl
from jax.experimental.pallas import tpu as pltpu

F32 = jnp.float32
BF16 = jnp.bfloat16

N_META = 16
CHUNK = 64
PAD = CHUNK - N_META
HEADS = 4
HEADS_PER_STEP = 2
DK = 128
DV = 256
GLA_RANK = 16
GLA_TAU = 16.0
ROPE_BASE = 10000.0
N_EXPERTS = 8
LN_EPS = 1e-5
GN_EPS = 1e-6

V7X_LANES = 128
V7X_BF16_SUBLANES = 16
V7X_MXU_COLS = 256
V7X_VMEM_LIMIT_BYTES = 56 * 1024 * 1024

SLAB = 8
ZERO_ROWS = 128

QK = HEADS * DK
VW = HEADS * DV
COL_GQ, COL_GK, COL_GV, COL_GR = 0, QK, 2 * QK, 2 * QK + VW
COL_RQ = 2 * QK + 2 * VW
COL_RK, COL_RV, COL_RG = COL_RQ + QK, COL_RQ + 2 * QK, COL_RQ + 2 * QK + VW
COL_GATE_A = COL_RQ + 2 * QK + 2 * VW
COL_GATE_B = COL_GATE_A + QK + QK
COL_GA = COL_GATE_B + QK + QK
N_PROJ = COL_GA + V7X_MXU_COLS


def _divisor_tile(n, cap, multiple):
    best = None
    for t in range(multiple, min(n, cap) + 1, multiple):
        if n % t == 0:
            best = t
    assert best is not None, (n, cap, multiple)
    return best


def _params(*sem):
    return pltpu.CompilerParams(dimension_semantics=sem, vmem_limit_bytes=V7X_VMEM_LIMIT_BYTES)


def _resident(shape):
    return pl.BlockSpec(shape, lambda *_: (0,) * len(shape), pipeline_mode=pl.Buffered(1))


def _layer_norm(r, g, b):
    mu = jnp.mean(r, axis=-1, keepdims=True)
    d = r - mu
    var = jnp.mean(d * d, axis=-1, keepdims=True)
    return d * lax.rsqrt(var + LN_EPS) * g + b


def _silu(x):
    return x * jax.nn.sigmoid(x)


def _in_proj_body(x_ref, keep_ref, w_ref, o_ref, xb_ref):
    @pl.when(pl.program_id(1) == 0)
    def _():
        xb_ref[...] = jnp.where(keep_ref[...] > 0.0, x_ref[...], 0.0).astype(BF16)

    xb = xb_ref[...]
    for c0 in range(0, o_ref.shape[1], V7X_MXU_COLS):
        cols = slice(c0, c0 + V7X_MXU_COLS)
        o_ref[:, cols] = jnp.dot(xb, w_ref[:, cols], preferred_element_type=F32).astype(BF16)


def _in_proj(h, keep, w):
    m, d = h.shape
    n = w.shape[1]
    tm = _divisor_tile(m, 1056, V7X_BF16_SUBLANES)
    tn = _divisor_tile(n, 2816, V7X_MXU_COLS)
    return pl.pallas_call(
        _in_proj_body,
        grid=(m // tm, n // tn),
        in_specs=[pl.BlockSpec((tm, d), lambda i, j: (i, 0)), pl.BlockSpec((tm, 1), lambda i, j: (i, 0)),
                  pl.BlockSpec((d, tn), lambda i, j: (0, j))],
        out_specs=pl.BlockSpec((tm, tn), lambda i, j: (i, j)),
        out_shape=jax.ShapeDtypeStruct((m, n), BF16),
        scratch_shapes=[pltpu.VMEM((tm, d), BF16)],
        compiler_params=_params("arbitrary", "arbitrary"),
        name="in_proj",
    )(h, keep, w)


def _mixer_body(gq_ref, gk_ref, gv_ref, gr_ref, ga_ref, rq_ref, rk_ref, rv_ref, rg_ref,
                wa2_ref, ba_ref, gng_ref, rng_ref, rnb_ref, cos_ref, sin_ref,
                dmat_ref, xi_ref, zeta_ref, rdec_ref,
                ya_ref, yb_ref, sa_ref, sb_ref, b_ref, oa_ref, ob_ref, *, n_groups, gsz, hps):
    c_ = CHUNK
    r_ = gsz * c_
    heads = range(hps)
    sa_ref[...] = jnp.zeros_like(sa_ref)
    sb_ref[...] = jnp.zeros_like(sb_ref)
    row = lax.broadcasted_iota(jnp.int32, (c_, c_), 0)
    col = lax.broadcasted_iota(jnp.int32, (c_, c_), 1)
    causal = row >= col
    grow = lax.broadcasted_iota(jnp.int32, (r_, r_), 0)
    gcol = lax.broadcasted_iota(jnp.int32, (r_, r_), 1)
    shift = int(math.log2(c_))
    gtri = jnp.where((jnp.right_shift(grow, shift) == jnp.right_shift(gcol, shift)) & (grow >= gcol),
                     1.0, 0.0).astype(BF16)
    wa2 = wa2_ref[...]
    ba = ba_ref[...]
    gng = gng_ref[...]
    rng = rng_ref[...]
    rnb = rnb_ref[...]
    nt = (((1,), (1,)), ((), ()))
    tn = (((0,), (0,)), ((), ()))
    mm = functools.partial(jnp.dot, preferred_element_type=F32)
    mmg = functools.partial(lax.dot_general, preferred_element_type=F32)
    chunks = [slice(g * c_, (g + 1) * c_) for g in range(gsz)]
    hk = lambda x, hh: x[:, hh * DK:(hh + 1) * DK]
    hv = lambda x, hh: x[:, hh * DV:(hh + 1) * DV]
    per_head = lambda xs: jnp.concatenate(xs, axis=1)

    def group_rows(j):
        return pl.ds(pl.multiple_of(j * r_, c_), r_)

    def gate_logits(j):
        return mm(ga_ref[0, group_rows(j), :], wa2) + ba

    def log_decay(z, j):
        valid = j * r_ + lax.broadcasted_iota(jnp.int32, (r_, 1), 0) >= PAD
        log_sig = jnp.minimum(z, 0.0) - jnp.log(1.0 + jnp.exp(-jnp.abs(z)))
        la = jnp.where(valid, log_sig * (1.0 / GLA_TAU), 0.0)
        la_hi = la.astype(BF16)
        la_lo = (la - la_hi.astype(F32)).astype(BF16)
        return mm(gtri, la_hi) + mm(gtri, la_lo)

    def finish(j, slot):
        rows = group_rows(j)
        o = oa_ref[slot]
        on = []
        for hh in heads:
            o_h = hv(o, hh)
            ms = jnp.mean(o_h * o_h, axis=-1, keepdims=True)
            on.append(o_h * lax.rsqrt(ms + LN_EPS) * gng)
        ya_ref[0, rows, :] = _silu(gr_ref[0, rows, :]) * per_head(on).astype(BF16)
        ob = ob_ref[slot]
        obn = []
        for hh in heads:
            ob_h = hv(ob, hh)
            dd = ob_h - jnp.mean(ob_h, axis=-1, keepdims=True)
            var = jnp.mean(dd * dd, axis=-1, keepdims=True)
            obn.append(dd * lax.rsqrt(var + GN_EPS))
        yb_ref[0, rows, :] = _silu(rg_ref[0, rows, :]) * (per_head(obn) * rng + rnb).astype(BF16)

    oa_ref[...] = jnp.zeros_like(oa_ref)
    ob_ref[...] = jnp.zeros_like(ob_ref)
    b_ref[0] = log_decay(gate_logits(0), 0)

    def group(gi, carry):
        slot = lax.rem(gi, 2)
        nxt = jnp.minimum(gi + 1, n_groups - 1)
        z_next = gate_logits(nxt)

        rows = group_rows(gi)
        cos = cos_ref[rows, :]
        sin = sin_ref[rows, :]
        rq = rq_ref[0, rows, :].astype(F32)
        rk = rk_ref[0, rows, :].astype(F32)
        vb = rv_ref[0, rows, :]
        qr, s_b, st_b = [], [], []
        for hh in heads:
            rq_h, rk_h = hk(rq, hh), hk(rk, hh)
            qr.append(rq_h * cos + pltpu.roll(rq_h, DK // 2, 1) * sin)
            kr = (rk_h * cos + pltpu.roll(rk_h, DK // 2, 1) * sin) * (DK ** -0.5)
            qr_b, kr_b = qr[hh].astype(BF16), kr.astype(BF16)
            s_b.append([(mmg(qr_b[cs], kr_b[cs], nt) * dmat_ref[hh]).astype(BF16) for cs in chunks])
            st, prev = sb_ref[hh], []
            for cs in chunks:
                prev.append(st.astype(BF16))
                st = st * rdec_ref[hh] + mmg((kr[cs] * zeta_ref[hh]).astype(BF16), hv(vb, hh)[cs], tn)
            sb_ref[hh] = st
            st_b.append(prev)

        b = b_ref[slot]
        q = gq_ref[0, rows, :].astype(F32) * (DK ** -0.5)
        k = gk_ref[0, rows, :].astype(F32)
        v = gv_ref[0, rows, :]
        q_in = (q * jnp.exp(b)).astype(BF16)
        k_in = (k * jnp.exp(-b)).astype(BF16)
        s_a, st_a = [], []
        for hh in heads:
            s_a.append([jnp.where(causal, mmg(hk(q_in, hh)[cs], hk(k_in, hh)[cs], nt), 0.0).astype(BF16)
                        for cs in chunks])
            st, prev = sa_ref[hh], []
            for cs in chunks:
                b_c = hk(b, hh)[cs]
                b_last = b_c[c_ - 1:c_, :]
                k_tail = (hk(k, hh)[cs] * jnp.exp(b_last - b_c)).astype(BF16)
                dcol = jnp.transpose(jnp.broadcast_to(jnp.exp(b_last), (DK, DK)))
                prev.append(st.astype(BF16))
                st = st * jnp.concatenate([dcol, dcol], axis=1) + mmg(k_tail, hv(v, hh)[cs], tn)
            sa_ref[hh] = st
            st_a.append(prev)

        b_ref[1 - slot] = log_decay(z_next, nxt)
        finish(jnp.maximum(gi - 1, 0), 1 - slot)

        o_a, o_b = [], []
        for hh in heads:
            o_b.append(jnp.concatenate(
                [mm(s_b[hh][g], hv(vb, hh)[cs]) + mm((qr[hh][cs] * xi_ref[hh]).astype(BF16), st_b[hh][g])
                 for g, cs in enumerate(chunks)], axis=0))
        for hh in heads:
            o_a.append(jnp.concatenate(
                [mm(s_a[hh][g], hv(v, hh)[cs]) + mm(hk(q_in, hh)[cs], st_a[hh][g])
                 for g, cs in enumerate(chunks)], axis=0))
        oa_ref[slot] = per_head(o_a)
        ob_ref[slot] = per_head(o_b)
        return carry

    lax.fori_loop(0, n_groups, group, 0)
    finish(n_groups - 1, (n_groups - 1) % 2)


def _retention_tables(lp):
    half = DK // 2
    pos = np.arange(lp, dtype=np.float64) - PAD
    inv = ROPE_BASE ** (-np.arange(half, dtype=np.float64) / half)
    ang = pos[:, None] * inv[None, :]
    cos = np.concatenate([np.cos(ang), np.cos(ang)], axis=1)
    sin = np.concatenate([-np.sin(ang), np.sin(ang)], axis=1)
    lg = np.log(1.0 - 2.0 ** (-5.0 - np.arange(HEADS, dtype=np.float64)))
    idx = np.arange(CHUNK, dtype=np.float64)
    rel = idx[:, None] - idx[None, :]
    dmat = np.where(rel >= 0, np.exp(lg[:, None, None] * np.maximum(rel, 0.0)), 0.0)
    xi = np.broadcast_to(np.exp(lg[:, None] * (idx + 1.0))[:, :, None], (HEADS, CHUNK, DK))
    zeta = np.broadcast_to(np.exp(lg[:, None] * (CHUNK - 1.0 - idx))[:, :, None], (HEADS, CHUNK, DK))
    rdec = np.broadcast_to(np.exp(lg * CHUNK)[:, None, None], (HEADS, 1, DV))
    f = lambda a: jnp.asarray(np.ascontiguousarray(a), F32)
    return f(cos), f(sin), f(dmat), f(xi), f(zeta), f(rdec)


def _mixer(proj3, wa2, ba, gng, rng, rnb, tables):
    bsz, lp, _ = proj3.shape
    cos, sin, dmat, xi, zeta, rdec = tables
    n_chunks = lp // CHUNK
    gsz = 3 if n_chunks % 3 == 0 else 1
    hps = HEADS_PER_STEP
    kw, vw = hps * DK, hps * DV
    qk = lambda col: pl.BlockSpec((1, lp, kw), lambda b, h, c=col // kw: (b, 0, c + h))
    vv = lambda col: pl.BlockSpec((1, lp, vw), lambda b, h, c=col // vw: (b, 0, c + h))
    in_specs = [
        qk(COL_GQ), qk(COL_GK), vv(COL_GV), vv(COL_GR),
        pl.BlockSpec((1, lp, V7X_LANES), lambda b, h: (b, 0, COL_GA // V7X_LANES)),
        qk(COL_RQ), qk(COL_RK), vv(COL_RV), vv(COL_RG),
        pl.BlockSpec((V7X_LANES, kw), lambda b, h: (0, h)),
        pl.BlockSpec((1, kw), lambda b, h: (0, h)),
        pl.BlockSpec((1, DV), lambda b, h: (0, 0)),
        pl.BlockSpec((1, vw), lambda b, h: (0, h)),
        pl.BlockSpec((1, vw), lambda b, h: (0, h)),
        pl.BlockSpec((lp, DK), lambda b, h: (0, 0)),
        pl.BlockSpec((lp, DK), lambda b, h: (0, 0)),
        pl.BlockSpec((hps, CHUNK, CHUNK), lambda b, h: (h, 0, 0)),
        pl.BlockSpec((hps, CHUNK, DK), lambda b, h: (h, 0, 0)),
        pl.BlockSpec((hps, CHUNK, DK), lambda b, h: (h, 0, 0)),
        pl.BlockSpec((hps, 1, DV), lambda b, h: (h, 0, 0)),
    ]
    out_spec = pl.BlockSpec((1, lp, vw), lambda b, h: (b, 0, h))
    return pl.pallas_call(
        functools.partial(_mixer_body, n_groups=n_chunks // gsz, gsz=gsz, hps=hps),
        grid=(bsz, HEADS // hps),
        in_specs=in_specs,
        out_specs=[out_spec, out_spec],
        out_shape=[jax.ShapeDtypeStruct((bsz, lp, VW), BF16)] * 2,
        scratch_shapes=[pltpu.VMEM((hps, DK, DV), F32), pltpu.VMEM((hps, DK, DV), F32),
                        pltpu.VMEM((2, gsz * CHUNK, kw), F32),
                        pltpu.VMEM((2, gsz * CHUNK, vw), F32), pltpu.VMEM((2, gsz * CHUNK, vw), F32)],
        compiler_params=_params("arbitrary", "arbitrary"),
        name="mixer",
    )(*([proj3] * 4), proj3, *([proj3] * 4), wa2, ba, gng, rng, rnb, cos, sin, dmat, xi, zeta, rdec)


def _mix_out_body(ya_ref, yb_ref, gta_ref, gtb_ref, h_ref, wpa_ref, wpb_ref, wo_ref,
                  bga_ref, bgb_ref, g_ref, b_ref, o_ref, *rest, alpha, with_slabs):
    mg_ref = rest[-1]
    ya = ya_ref[...]
    yb = yb_ref[...]
    for c0 in range(0, mg_ref.shape[1], V7X_MXU_COLS):
        cols = slice(c0, c0 + V7X_MXU_COLS)
        y_a = jnp.dot(ya, wpa_ref[:, cols], preferred_element_type=F32)
        y_b = jnp.dot(yb, wpb_ref[:, cols], preferred_element_type=F32)
        mg_ref[:, cols] = (jax.nn.sigmoid(gta_ref[:, cols].astype(F32) + bga_ref[:, cols]) * y_a
                           + jax.nn.sigmoid(gtb_ref[:, cols].astype(F32) + bgb_ref[:, cols]) * y_b).astype(BF16)
    mix = jnp.dot(mg_ref[...], wo_ref[...], preferred_element_type=F32)
    out = _layer_norm(alpha * h_ref[...] + mix, g_ref[...], b_ref[...])
    o_ref[...] = out
    if with_slabs:
        hs_ref = rest[0]
        for g in range(SLAB):
            hs_ref[pl.ds(g, out.shape[0], stride=SLAB), :] = out[:, g * V7X_LANES:(g + 1) * V7X_LANES]


def _mix_out(ya, yb, proj, h, wpa, wpb, wo, bga, bgb, g, b, alpha, with_slabs):
    m, d = h.shape
    tm = _divisor_tile(m, 1056, V7X_BF16_SUBLANES)
    rows = lambda: pl.BlockSpec((tm, d), lambda i: (i, 0))
    out_specs = [rows()]
    out_shape = [jax.ShapeDtypeStruct((m, d), F32)]
    if with_slabs:
        assert d == SLAB * V7X_LANES
        out_specs.append(pl.BlockSpec((tm * SLAB, V7X_LANES), lambda i: (i, 0)))
        out_shape.append(jax.ShapeDtypeStruct((m * SLAB, V7X_LANES), F32))
    return pl.pallas_call(
        functools.partial(_mix_out_body, alpha=alpha, with_slabs=with_slabs),
        grid=(m // tm,),
        in_specs=[rows(), rows(),
                  pl.BlockSpec((tm, d), lambda i: (i, COL_GATE_A // d)),
                  pl.BlockSpec((tm, d), lambda i: (i, COL_GATE_B // d)),
                  rows(), _resident((d, d)), _resident((d, d)), _resident((d, d)),
                  _resident((1, d)), _resident((1, d)), _resident((1, d)), _resident((1, d))],
        out_specs=out_specs,
        out_shape=out_shape,
        scratch_shapes=[pltpu.VMEM((tm, d), BF16)],
        compiler_params=_params("arbitrary"),
        name="mix_out",
    )(ya, yb, proj, proj, h, wpa, wpb, wo, bga, bgb, g, b)


def _swiglu_hidden(xb, wg_ref, wu_ref, hid_ref, widx=()):
    dff = hid_ref.shape[1]
    for c0 in range(0, dff, V7X_MXU_COLS):
        cols = slice(c0, c0 + V7X_MXU_COLS)
        gate = jnp.dot(xb, wg_ref[widx + (slice(None), cols)], preferred_element_type=F32)
        up = jnp.dot(xb, wu_ref[widx + (slice(None), cols)], preferred_element_type=F32)
        hid_ref[:, cols] = (_silu(gate) * up).astype(BF16)


def _ffn_body(x_ref, wg_ref, wu_ref, wd_ref, g_ref, b_ref, o_ref, hid_ref, *, alpha):
    x = x_ref[...]
    _swiglu_hidden(x.astype(BF16), wg_ref, wu_ref, hid_ref)
    f = jnp.dot(hid_ref[...], wd_ref[...], preferred_element_type=F32)
    o_ref[...] = _layer_norm(alpha * x + f, g_ref[...], b_ref[...])


def _ffn(h, wg, wu, wd, g, b, alpha):
    m, d = h.shape
    dff = wg.shape[1]
    assert dff % V7X_MXU_COLS == 0
    tm = _divisor_tile(m, 1056, V7X_BF16_SUBLANES)
    return pl.pallas_call(
        functools.partial(_ffn_body, alpha=alpha),
        grid=(m // tm,),
        in_specs=[pl.BlockSpec((tm, d), lambda i: (i, 0)),
                  _resident((d, dff)), _resident((d, dff)), _resident((dff, d)),
                  _resident((1, d)), _resident((1, d))],
        out_specs=pl.BlockSpec((tm, d), lambda i: (i, 0)),
        out_shape=jax.ShapeDtypeStruct((m, d), F32),
        scratch_shapes=[pltpu.VMEM((tm, dff), BF16)],
        compiler_params=_params("arbitrary"),
        name="ffn",
    )(h, wg, wu, wd, g, b)


def _router_body(h_ref, wr_ref, idx_ref, wt_ref):
    logits = lax.dot_general(wr_ref[...], h_ref[...], (((1,), (1,)), ((), ())),
                             precision=lax.Precision.HIGHEST, preferred_element_type=F32)
    e = lax.broadcasted_iota(jnp.int32, logits.shape, 0)
    m1 = jnp.max(logits, axis=0, keepdims=True)
    i1 = jnp.min(jnp.where(logits == m1, e, N_EXPERTS), axis=0, keepdims=True)
    rest = jnp.where(e == i1, -jnp.inf, logits)
    m2 = jnp.max(rest, axis=0, keepdims=True)
    i2 = jnp.min(jnp.where(rest == m2, e, N_EXPERTS), axis=0, keepdims=True)
    e2 = jnp.exp(m2 - m1)
    w1 = 1.0 / (1.0 + e2)
    idx_ref[0:1, :] = i1
    idx_ref[1:2, :] = i2
    wt_ref[0:1, :] = w1
    wt_ref[1:2, :] = e2 * w1


def _router(h, wr_t):
    m, d = h.shape
    tm = _divisor_tile(m, 1024, V7X_LANES)
    return pl.pallas_call(
        _router_body,
        grid=(m // tm,),
        in_specs=[pl.BlockSpec((tm, d), lambda i: (i, 0)), pl.BlockSpec((N_EXPERTS, d), lambda i: (0, 0))],
        out_specs=[pl.BlockSpec((2, tm), lambda i: (0, i)), pl.BlockSpec((2, tm), lambda i: (0, i))],
        out_shape=[jax.ShapeDtypeStruct((2, m), jnp.int32), jax.ShapeDtypeStruct((2, m), F32)],
        compiler_params=_params("arbitrary"),
        name="router",
    )(h, wr_t)


def _index_prefetch(idx_hbm, idx_smem, isem, i, n):
    slot = lax.rem(i, 2)

    @pl.when(i == 0)
    def _():
        pltpu.make_async_copy(idx_hbm.at[0], idx_smem.at[0], isem.at[0]).start()

    pltpu.make_async_copy(idx_hbm.at[i], idx_smem.at[slot], isem.at[slot]).wait()

    @pl.when(i + 1 < n)
    def _():
        pltpu.make_async_copy(idx_hbm.at[i + 1], idx_smem.at[1 - slot], isem.at[1 - slot]).start()

    return slot


def _slab_rows(p):
    return pl.ds(pl.multiple_of(p * SLAB, SLAB), SLAB)


def _slabs_to_matrix(ref2, rows):
    return jnp.concatenate([ref2[pl.ds(g, rows, stride=SLAB), :] for g in range(SLAB)], axis=1)


def _dispatch_body(st_ref, cnt_ref, pd_ref, pos_hbm, h3_ref, xs_hbm, idx_smem, zbuf, isem, sem, psem,
                   *, tm, n_rows):
    i = pl.program_id(0)
    n = pl.num_programs(0)
    slot = _index_prefetch(pos_hbm, idx_smem, isem, i, n)

    def issue(r, carry):
        p0 = idx_smem[slot, r]
        p1 = idx_smem[slot, tm + r]
        pltpu.make_async_copy(h3_ref.at[_slab_rows(r)], xs_hbm.at[_slab_rows(p0)], sem).start()
        pltpu.make_async_copy(h3_ref.at[_slab_rows(r)], xs_hbm.at[_slab_rows(p1)], sem).start(priority=1)
        return carry

    lax.fori_loop(0, tm, issue, 0, unroll=8)
    pltpu.make_async_copy(h3_ref, xs_hbm.at[pl.ds(0, tm * SLAB)], sem).wait()
    pltpu.make_async_copy(h3_ref, xs_hbm.at[pl.ds(0, tm * SLAB)], sem).wait()

    @pl.when(i == n - 1)
    def _():
        zbuf[...] = jnp.zeros_like(zbuf)
        zrow = zbuf.at[pl.ds(0, SLAB)]
        for e in range(N_EXPERTS):
            lo = st_ref[e] + cnt_ref[e]
            npad = pd_ref[e] - cnt_ref[e]

            def fill(r, carry):
                pltpu.make_async_copy(zrow, xs_hbm.at[_slab_rows(lo + r)], psem).start()
                return carry

            def drain(r, carry):
                pltpu.make_async_copy(zrow, xs_hbm.at[_slab_rows(lo + r)], psem).wait()
                return carry

            lax.fori_loop(0, npad, fill, 0)
            lax.fori_loop(0, npad, drain, 0)

        zr = zbuf.shape[0] // SLAB
        tail = st_ref[N_EXPERTS - 1] + pd_ref[N_EXPERTS - 1]
        nblk = (n_rows - tail) // zr

        def block(k):
            return xs_hbm.at[pl.ds(pl.multiple_of((tail + k * zr) * SLAB, SLAB), zr * SLAB)]

        def fill_block(k, carry):
            pltpu.make_async_copy(zbuf, block(k), psem).start()
            return carry

        def drain_block(k, carry):
            pltpu.make_async_copy(zbuf, block(k), psem).wait()
            return carry

        lax.fori_loop(0, nblk, fill_block, 0)
        lax.fori_loop(0, nblk, drain_block, 0)


def _dispatch(h3, pos2d, starts, counts, padded, n_rows, tm):
    m = h3.shape[0] // SLAB
    grid_spec = pltpu.PrefetchScalarGridSpec(
        num_scalar_prefetch=3,
        grid=(m // tm,),
        in_specs=[pl.BlockSpec(memory_space=pl.ANY),
                  pl.BlockSpec((tm * SLAB, V7X_LANES), lambda i, *_: (i, 0))],
        out_specs=pl.BlockSpec(memory_space=pl.ANY),
        scratch_shapes=[pltpu.SMEM((2, 2 * tm), jnp.int32), pltpu.VMEM((ZERO_ROWS * SLAB, V7X_LANES), F32),
                        pltpu.SemaphoreType.DMA((2,)), pltpu.SemaphoreType.DMA, pltpu.SemaphoreType.DMA],
    )
    return pl.pallas_call(
        functools.partial(_dispatch_body, tm=tm, n_rows=n_rows),
        grid_spec=grid_spec,
        out_shape=jax.ShapeDtypeStruct((n_rows * SLAB, V7X_LANES), F32),
        compiler_params=_params("arbitrary"),
        name="moe_dispatch",
    )(starts, counts, padded, pos2d, h3)


def _moe_body(te_ref, nu_ref, x_ref, wg_ref, wu_ref, wd_ref, o_ref, hid_ref):
    rows = hid_ref.shape[0]

    @pl.when(pl.program_id(0) < nu_ref[0])
    def _():
        xb = _slabs_to_matrix(x_ref, rows).astype(BF16)
        _swiglu_hidden(xb, wg_ref, wu_ref, hid_ref, widx=(0,))
        y = jnp.dot(hid_ref[...], wd_ref[0], preferred_element_type=F32)
        for g in range(SLAB):
            o_ref[pl.ds(g, rows, stride=SLAB), :] = y[:, g * V7X_LANES:(g + 1) * V7X_LANES]

    @pl.when(pl.program_id(0) >= nu_ref[0])
    def _():
        o_ref[...] = jnp.zeros_like(o_ref)


def _moe_experts(xs, tile_expert, n_used, wg, wu, wd, tme):
    r = xs.shape[0] // SLAB
    d, dff = wg.shape[1], wg.shape[2]
    assert dff % V7X_MXU_COLS == 0
    expert = lambda shape: pl.BlockSpec((1,) + shape, lambda i, te, nu: (te[i], 0, 0), pipeline_mode=pl.Buffered(1))
    grid_spec = pltpu.PrefetchScalarGridSpec(
        num_scalar_prefetch=2,
        grid=(r // tme,),
        in_specs=[pl.BlockSpec((tme * SLAB, V7X_LANES), lambda i, te, nu: (i, 0)),
                  expert((d, dff)), expert((d, dff)), expert((dff, d))],
        out_specs=pl.BlockSpec((tme * SLAB, V7X_LANES), lambda i, te, nu: (i, 0)),
        scratch_shapes=[pltpu.VMEM((tme, dff), BF16)],
    )
    return pl.pallas_call(
        _moe_body,
        grid_spec=grid_spec,
        out_shape=jax.ShapeDtypeStruct(xs.shape, F32),
        compiler_params=_params("arbitrary"),
        name="moe_experts",
    )(tile_expert, n_used, xs, wg, wu, wd)


def _combine_body(pos_hbm, ys_hbm, wt_ref, h_ref, g_ref, b_ref, o_ref, ybuf, idx_smem, isem, sem, *, tm, alpha):
    i = pl.program_id(0)
    slot = _index_prefetch(pos_hbm, idx_smem, isem, i, pl.num_programs(0))

    def issue(r, carry):
        p0 = idx_smem[slot, r]
        p1 = idx_smem[slot, tm + r]
        pltpu.make_async_copy(ys_hbm.at[_slab_rows(p0)], ybuf.at[0, _slab_rows(r)], sem).start()
        pltpu.make_async_copy(ys_hbm.at[_slab_rows(p1)], ybuf.at[1, _slab_rows(r)], sem).start(priority=1)
        return carry

    lax.fori_loop(0, tm, issue, 0, unroll=8)
    pltpu.make_async_copy(ys_hbm.at[pl.ds(0, tm * SLAB)], ybuf.at[0], sem).wait()
    pltpu.make_async_copy(ys_hbm.at[pl.ds(0, tm * SLAB)], ybuf.at[1], sem).wait()
    wt = wt_ref[...]
    out = wt[:, 0:1] * _slabs_to_matrix(ybuf.at[0], tm) + wt[:, 1:2] * _slabs_to_matrix(ybuf.at[1], tm)
    o_ref[...] = _layer_norm(alpha * h_ref[...] + out, g_ref[...], b_ref[...])


def _combine(pos2d, ys, wt, h, g, b, alpha, tm):
    m, d = h.shape
    return pl.pallas_call(
        functools.partial(_combine_body, tm=tm, alpha=alpha),
        grid=(m // tm,),
        in_specs=[pl.BlockSpec(memory_space=pl.ANY), pl.BlockSpec(memory_space=pl.ANY),
                  pl.BlockSpec((tm, 2), lambda i: (i, 0)),
                  pl.BlockSpec((tm, d), lambda i: (i, 0)),
                  pl.BlockSpec((1, d), lambda i: (0, 0)), pl.BlockSpec((1, d), lambda i: (0, 0))],
        out_specs=pl.BlockSpec((tm, d), lambda i: (i, 0)),
        out_shape=jax.ShapeDtypeStruct((m, d), F32),
        scratch_shapes=[pltpu.VMEM((2, tm * SLAB, V7X_LANES), F32), pltpu.SMEM((2, 2 * tm), jnp.int32),
                        pltpu.SemaphoreType.DMA((2,)), pltpu.SemaphoreType.DMA],
        compiler_params=_params("arbitrary"),
        name="moe_combine",
    )(pos2d, ys, wt, h, g, b)


def _moe_layer(h, h_slabs, wr, wg, wu, wd, g, b, alpha):
    m, d = h.shape
    assert d == SLAB * V7X_LANES
    tme = _divisor_tile(m, 1024, ZERO_ROWS)
    tmc = _divisor_tile(m, 512, 8)
    idx, wts = _router(h, wr.T)

    e_flat = idx.reshape(-1)
    onehot = (e_flat[:, None] == jnp.arange(N_EXPERTS, dtype=jnp.int32)[None, :]).astype(jnp.int32)
    csum = jnp.cumsum(onehot, axis=0)
    rank = jnp.take_along_axis(csum, e_flat[:, None], axis=1)[:, 0] - 1
    counts = csum[-1]
    padded = ((counts + tme - 1) // tme) * tme
    ends = jnp.cumsum(padded)
    starts = ends - padded
    pos = starts[e_flat] + rank
    n_rows = 2 * m + N_EXPERTS * tme
    n_tiles = n_rows // tme
    tile_start = jnp.arange(n_tiles, dtype=jnp.int32) * tme
    tile_expert = jnp.minimum(jnp.sum(tile_start[:, None] >= ends[None, :], axis=1), N_EXPERTS - 1).astype(jnp.int32)
    n_used = (ends[-1] // tme).astype(jnp.int32).reshape(1)
    pos2d = pos.reshape(2, m // tmc, tmc).transpose(1, 0, 2).reshape(m // tmc, 2 * tmc)

    xs = _dispatch(h_slabs, pos2d, starts, counts, padded, n_rows, tmc)
    ys = _moe_experts(xs, tile_expert, n_used, wg, wu, wd, tme)
    return _combine(pos2d, ys, wts.T, h, g, b, alpha, tmc)


def kernel(x, meta, w_in, b_gate, gla_w_a2, gla_b_a, gla_norm_g, ret_norm_g, ret_norm_b, w_pa, w_pb, w_o,
           ln1_g, ln1_b, ffn_w_gate, ffn_w_up, ffn_w_down, moe_w_router, moe_w_gate, moe_w_up, moe_w_down,
           ln2_g, ln2_b):
    bsz, seq, d = x.shape
    depth = w_in.shape[0]
    alpha = (2 * depth) ** 0.25
    lp = seq + N_META + PAD
    assert lp % CHUNK == 0 and d == VW
    m = bsz * lp

    h = jnp.concatenate([jnp.zeros((bsz, PAD, d), x.dtype),
                         jnp.broadcast_to(meta[None].astype(x.dtype), (bsz, N_META, d)), x], axis=1)
    h = h.reshape(m, d)
    tables = _retention_tables(lp)
    keep = jnp.asarray(np.tile((np.arange(lp) >= PAD).astype(np.float32), bsz).reshape(m, 1))

    c0 = 2 * QK + 2 * VW
    row = lambda a: a.reshape(1, -1)
    for l in range(depth):
        w = w_in[l]
        w_perm = jnp.concatenate(
            [w[:, :c0], w[:, c0 + GLA_RANK:], w[:, c0:c0 + GLA_RANK],
             jnp.zeros((d, V7X_MXU_COLS - GLA_RANK), w.dtype)], axis=1).astype(BF16)
        wa2 = jnp.concatenate([gla_w_a2[l], jnp.zeros((V7X_LANES - GLA_RANK, QK), F32)], axis=0).astype(BF16)

        proj = _in_proj(h, keep, w_perm)
        ya, yb = _mixer(proj.reshape(bsz, lp, N_PROJ), wa2, row(gla_b_a[l]), row(gla_norm_g[l]),
                        row(ret_norm_g[l]), row(ret_norm_b[l]), tables)
        is_moe = l % 2 == 1
        mixed = _mix_out(ya.reshape(m, VW), yb.reshape(m, VW), proj, h,
                         w_pa[l].astype(BF16), w_pb[l].astype(BF16), w_o[l].astype(BF16),
                         row(b_gate[l, 0]), row(b_gate[l, 1]), row(ln1_g[l]), row(ln1_b[l]), alpha, is_moe)
        h = mixed[0]
        if l % 2 == 0:
            j = l // 2
            h = _ffn(h, ffn_w_gate[j].astype(BF16), ffn_w_up[j].astype(BF16), ffn_w_down[j].astype(BF16),
                     row(ln2_g[l]), row(ln2_b[l]), alpha)
        else:
            j = l // 2
            h = _moe_layer(h, mixed[1], moe_w_router[j], moe_w_gate[j].astype(BF16), moe_w_up[j].astype(BF16),
                           moe_w_down[j].astype(BF16), row(ln2_g[l]), row(ln2_b[l]), alpha)
    return h.reshape(bsz, lp, d)[:, PAD + N_META:]
```

```python
import functools
import math

import jax
import jax.numpy as jnp
import numpy as np
from jax import lax
from jax.experimental import pallas as pl
from jax.experimental.pallas import tpu as pltpu

F32 = jnp.float32
BF16 = jnp.bfloat16

N_META = 16
CHUNK = 64
PAD = CHUNK - N_META
HEADS = 4
HEADS_PER_STEP = 2
DK = 128
DV = 256
GLA_RANK = 16
GLA_TAU = 16.0
ROPE_BASE = 10000.0
N_EXPERTS = 8
LN_EPS = 1e-5
GN_EPS = 1e-6

V7X_LANES = 128
V7X_BF16_SUBLANES = 16
V7X_MXU_COLS = 256
V7X_VMEM_LIMIT_BYTES = 56 * 1024 * 1024

SLAB = 8
ZERO_ROWS = 128

QK = HEADS * DK
VW = HEADS * DV
COL_GQ, COL_GK, COL_GV, COL_GR = 0, QK, 2 * QK, 2 * QK + VW
COL_RQ = 2 * QK + 2 * VW
COL_RK, COL_RV, COL_RG = COL_RQ + QK, COL_RQ + 2 * QK, COL_RQ + 2 * QK + VW
COL_GATE_A = COL_RQ + 2 * QK + 2 * VW
COL_GATE_B = COL_GATE_A + QK + QK
COL_GA = COL_GATE_B + QK + QK
N_PROJ = COL_GA + V7X_MXU_COLS


def _divisor_tile(n, cap, multiple):
    best = None
    for t in range(multiple, min(n, cap) + 1, multiple):
        if n % t == 0:
            best = t
    assert best is not None, (n, cap, multiple)
    return best


def _params(*sem):
    return pltpu.CompilerParams(dimension_semantics=sem, vmem_limit_bytes=V7X_VMEM_LIMIT_BYTES)


def _resident(shape):
    return pl.BlockSpec(shape, lambda *_: (0,) * len(shape), pipeline_mode=pl.Buffered(1))


def _layer_norm(r, g, b):
    mu = jnp.mean(r, axis=-1, keepdims=True)
    d = r - mu
    var = jnp.mean(d * d, axis=-1, keepdims=True)
    return d * lax.rsqrt(var + LN_EPS) * g + b


def _silu(x):
    return x * jax.nn.sigmoid(x)


def _in_proj_body(x_ref, keep_ref, w_ref, o_ref, xb_ref):
    @pl.when(pl.program_id(1) == 0)
    def _():
        xb_ref[...] = jnp.where(keep_ref[...] > 0.0, x_ref[...], 0.0).astype(BF16)

    xb = xb_ref[...]
    for c0 in range(0, o_ref.shape[1], V7X_MXU_COLS):
        cols = slice(c0, c0 + V7X_MXU_COLS)
        o_ref[:, cols] = jnp.dot(xb, w_ref[:, cols], preferred_element_type=F32).astype(BF16)


def _in_proj(h, keep, w):
    m, d = h.shape
    n = w.shape[1]
    tm = _divisor_tile(m, 1056, V7X_BF16_SUBLANES)
    tn = _divisor_tile(n, 2816, V7X_MXU_COLS)
    return pl.pallas_call(
        _in_proj_body,
        grid=(m // tm, n // tn),
        in_specs=[pl.BlockSpec((tm, d), lambda i, j: (i, 0)), pl.BlockSpec((tm, 1), lambda i, j: (i, 0)),
                  pl.BlockSpec((d, tn), lambda i, j: (0, j))],
        out_specs=pl.BlockSpec((tm, tn), lambda i, j: (i, j)),
        out_shape=jax.ShapeDtypeStruct((m, n), BF16),
        scratch_shapes=[pltpu.VMEM((tm, d), BF16)],
        compiler_params=_params("arbitrary", "arbitrary"),
        name="in_proj",
    )(h, keep, w)


def _mixer_body(gq_ref, gk_ref, gv_ref, gr_ref, ga_ref, rq_ref, rk_ref, rv_ref, rg_ref,
                wa2_ref, ba_ref, gng_ref, rng_ref, rnb_ref, cos_ref, sin_ref,
                dmat_ref, xi_ref, zeta_ref, rdec_ref,
                ya_ref, yb_ref, sa_ref, sb_ref, b_ref, oa_ref, ob_ref, *, n_groups, gsz, hps):
    c_ = CHUNK
    r_ = gsz * c_
    heads = range(hps)
    sa_ref[...] = jnp.zeros_like(sa_ref)
    sb_ref[...] = jnp.zeros_like(sb_ref)
    row = lax.broadcasted_iota(jnp.int32, (c_, c_), 0)
    col = lax.broadcasted_iota(jnp.int32, (c_, c_), 1)
    causal = row >= col
    grow = lax.broadcasted_iota(jnp.int32, (r_, r_), 0)
    gcol = lax.broadcasted_iota(jnp.int32, (r_, r_), 1)
    shift = int(math.log2(c_))
    gtri = jnp.where((jnp.right_shift(grow, shift) == jnp.right_shift(gcol, shift)) & (grow >= gcol),
                     1.0, 0.0).astype(BF16)
    wa2 = wa2_ref[...]
    ba = ba_ref[...]
    gng = gng_ref[...]
    rng = rng_ref[...]
    rnb = rnb_ref[...]
    nt = (((1,), (1,)), ((), ()))
    tn = (((0,), (0,)), ((), ()))
    mm = functools.partial(jnp.dot, preferred_element_type=F32)
    mmg = functools.partial(lax.dot_general, preferred_element_type=F32)
    chunks = [slice(g * c_, (g + 1) * c_) for g in range(gsz)]
    hk = lambda x, hh: x[:, hh * DK:(hh + 1) * DK]
    hv = lambda x, hh: x[:, hh * DV:(hh + 1) * DV]
    per_head = lambda xs: jnp.concatenate(xs, axis=1)

    def group_rows(j):
        return pl.ds(pl.multiple_of(j * r_, c_), r_)

    def gate_logits(j):
        return mm(ga_ref[0, group_rows(j), :], wa2) + ba

    def log_decay(z, j):
        valid = j * r_ + lax.broadcasted_iota(jnp.int32, (r_, 1), 0) >= PAD
        log_sig = jnp.minimum(z, 0.0) - jnp.log(1.0 + jnp.exp(-jnp.abs(z)))
        la = jnp.where(valid, log_sig * (1.0 / GLA_TAU), 0.0)
        la_hi = la.astype(BF16)
        la_lo = (la - la_hi.astype(F32)).astype(BF16)
        return mm(gtri, la_hi) + mm(gtri, la_lo)

    def finish(j, slot):
        rows = group_rows(j)
        o = oa_ref[slot]
        on = []
        for hh in heads:
            o_h = hv(o, hh)
            ms = jnp.mean(o_h * o_h, axis=-1, keepdims=True)
            on.append(o_h * lax.rsqrt(ms + LN_EPS) * gng)
        ya_ref[0, rows, :] = _silu(gr_ref[0, rows, :]) * per_head(on).astype(BF16)
        ob = ob_ref[slot]
        obn = []
        for hh in heads:
            ob_h = hv(ob, hh)
            dd = ob_h - jnp.mean(ob_h, axis=-1, keepdims=True)
            var = jnp.mean(dd * dd, axis=-1, keepdims=True)
            obn.append(dd * lax.rsqrt(var + GN_EPS))
        yb_ref[0, rows, :] = _silu(rg_ref[0, rows, :]) * (per_head(obn) * rng + rnb).astype(BF16)

    oa_ref[...] = jnp.zeros_like(oa_ref)
    ob_ref[...] = jnp.zeros_like(ob_ref)
    b_ref[0] = log_decay(gate_logits(0), 0)

    def group(gi, carry):
        slot = lax.rem(gi, 2)
        nxt = jnp.minimum(gi + 1, n_groups - 1)
        z_next = gate_logits(nxt)

        rows = group_rows(gi)
        cos = cos_ref[rows, :]
        sin = sin_ref[rows, :]
        rq = rq_ref[0, rows, :].astype(F32)
        rk = rk_ref[0, rows, :].astype(F32)
        vb = rv_ref[0, rows, :]
        qr, s_b, st_b = [], [], []
        for hh in heads:
            rq_h, rk_h = hk(rq, hh), hk(rk, hh)
            qr.append(rq_h * cos + pltpu.roll(rq_h, DK // 2, 1) * sin)
            kr = (rk_h * cos + pltpu.roll(rk_h, DK // 2, 1) * sin) * (DK ** -0.5)
            qr_b, kr_b = qr[hh].astype(BF16), kr.astype(BF16)
            s_b.append([(mmg(qr_b[cs], kr_b[cs], nt) * dmat_ref[hh]).astype(BF16) for cs in chunks])
            st, prev = sb_ref[hh], []
            for cs in chunks:
                prev.append(st.astype(BF16))
                st = st * rdec_ref[hh] + mmg((kr[cs] * zeta_ref[hh]).astype(BF16), hv(vb, hh)[cs], tn)
            sb_ref[hh] = st
            st_b.append(prev)

        b = b_ref[slot]
        q = gq_ref[0, rows, :].astype(F32) * (DK ** -0.5)
        k = gk_ref[0, rows, :].astype(F32)
        v = gv_ref[0, rows, :]
        q_in = (q * jnp.exp(b)).astype(BF16)
        k_in = (k * jnp.exp(-b)).astype(BF16)
        s_a, st_a = [], []
        for hh in heads:
            s_a.append([jnp.where(causal, mmg(hk(q_in, hh)[cs], hk(k_in, hh)[cs], nt), 0.0).astype(BF16)
                        for cs in chunks])
            st, prev = sa_ref[hh], []
            for cs in chunks:
                b_c = hk(b, hh)[cs]
                b_last = b_c[c_ - 1:c_, :]
                k_tail = (hk(k, hh)[cs] * jnp.exp(b_last - b_c)).astype(BF16)
                dcol = jnp.transpose(jnp.broadcast_to(jnp.exp(b_last), (DK, DK)))
                prev.append(st.astype(BF16))
                st = st * jnp.concatenate([dcol, dcol], axis=1) + mmg(k_tail, hv(v, hh)[cs], tn)
            sa_ref[hh] = st
            st_a.append(prev)

        b_ref[1 - slot] = log_decay(z_next, nxt)
        finish(jnp.maximum(gi - 1, 0), 1 - slot)

        o_a, o_b = [], []
        for hh in heads:
            o_b.append(jnp.concatenate(
                [mm(s_b[hh][g], hv(vb, hh)[cs]) + mm((qr[hh][cs] * xi_ref[hh]).astype(BF16), st_b[hh][g])
                 for g, cs in enumerate(chunks)], axis=0))
        for hh in heads:
            o_a.append(jnp.concatenate(
                [mm(s_a[hh][g], hv(v, hh)[cs]) + mm(hk(q_in, hh)[cs], st_a[hh][g])
                 for g, cs in enumerate(chunks)], axis=0))
        oa_ref[slot] = per_head(o_a)
        ob_ref[slot] = per_head(o_b)
        return carry

    lax.fori_loop(0, n_groups, group, 0)
    finish(n_groups - 1, (n_groups - 1) % 2)


def _retention_tables(lp):
    half = DK // 2
    pos = np.arange(lp, dtype=np.float64) - PAD
    inv = ROPE_BASE ** (-np.arange(half, dtype=np.float64) / half)
    ang = pos[:, None] * inv[None, :]
    cos = np.concatenate([np.cos(ang), np.cos(ang)], axis=1)
    sin = np.concatenate([-np.sin(ang), np.sin(ang)], axis=1)
    lg = np.log(1.0 - 2.0 ** (-5.0 - np.arange(HEADS, dtype=np.float64)))
    idx = np.arange(CHUNK, dtype=np.float64)
    rel = idx[:, None] - idx[None, :]
    dmat = np.where(rel >= 0, np.exp(lg[:, None, None] * np.maximum(rel, 0.0)), 0.0)
    xi = np.broadcast_to(np.exp(lg[:, None] * (idx + 1.0))[:, :, None], (HEADS, CHUNK, DK))
    zeta = np.broadcast_to(np.exp(lg[:, None] * (CHUNK - 1.0 - idx))[:, :, None], (HEADS, CHUNK, DK))
    rdec = np.broadcast_to(np.exp(lg * CHUNK)[:, None, None], (HEADS, 1, DV))
    f = lambda a: jnp.asarray(np.ascontiguousarray(a), F32)
    return f(cos), f(sin), f(dmat), f(xi), f(zeta), f(rdec)


def _mixer(proj3, wa2, ba, gng, rng, rnb, tables):
    bsz, lp, _ = proj3.shape
    cos, sin, dmat, xi, zeta, rdec = tables
    n_chunks = lp // CHUNK
    gsz = 3 if n_chunks % 3 == 0 else 1
    hps = HEADS_PER_STEP
    kw, vw = hps * DK, hps * DV
    qk = lambda col: pl.BlockSpec((1, lp, kw), lambda b, h, c=col // kw: (b, 0, c + h))
    vv = lambda col: pl.BlockSpec((1, lp, vw), lambda b, h, c=col // vw: (b, 0, c + h))
    in_specs = [
        qk(COL_GQ), qk(COL_GK), vv(COL_GV), vv(COL_GR),
        pl.BlockSpec((1, lp, V7X_LANES), lambda b, h: (b, 0, COL_GA // V7X_LANES)),
        qk(COL_RQ), qk(COL_RK), vv(COL_RV), vv(COL_RG),
        pl.BlockSpec((V7X_LANES, kw), lambda b, h: (0, h)),
        pl.BlockSpec((1, kw), lambda b, h: (0, h)),
        pl.BlockSpec((1, DV), lambda b, h: (0, 0)),
        pl.BlockSpec((1, vw), lambda b, h: (0, h)),
        pl.BlockSpec((1, vw), lambda b, h: (0, h)),
        pl.BlockSpec((lp, DK), lambda b, h: (0, 0)),
        pl.BlockSpec((lp, DK), lambda b, h: (0, 0)),
        pl.BlockSpec((hps, CHUNK, CHUNK), lambda b, h: (h, 0, 0)),
        pl.BlockSpec((hps, CHUNK, DK), lambda b, h: (h, 0, 0)),
        pl.BlockSpec((hps, CHUNK, DK), lambda b, h: (h, 0, 0)),
        pl.BlockSpec((hps, 1, DV), lambda b, h: (h, 0, 0)),
    ]
    out_spec = pl.BlockSpec((1, lp, vw), lambda b, h: (b, 0, h))
    return pl.pallas_call(
        functools.partial(_mixer_body, n_groups=n_chunks // gsz, gsz=gsz, hps=hps),
        grid=(bsz, HEADS // hps),
        in_specs=in_specs,
        out_specs=[out_spec, out_spec],
        out_shape=[jax.ShapeDtypeStruct((bsz, lp, VW), BF16)] * 2,
        scratch_shapes=[pltpu.VMEM((hps, DK, DV), F32), pltpu.VMEM((hps, DK, DV), F32),
                        pltpu.VMEM((2, gsz * CHUNK, kw), F32),
                        pltpu.VMEM((2, gsz * CHUNK, vw), F32), pltpu.VMEM((2, gsz * CHUNK, vw), F32)],
        compiler_params=_params("arbitrary", "arbitrary"),
        name="mixer",
    )(*([proj3] * 4), proj3, *([proj3] * 4), wa2, ba, gng, rng, rnb, cos, sin, dmat, xi, zeta, rdec)


def _mix_out_body(ya_ref, yb_ref, gta_ref, gtb_ref, h_ref, wpa_ref, wpb_ref, wo_ref,
                  bga_ref, bgb_ref, g_ref, b_ref, o_ref, *rest, alpha, with_slabs):
    mg_ref = rest[-1]
    ya = ya_ref[...]
    yb = yb_ref[...]
    for c0 in range(0, mg_ref.shape[1], V7X_MXU_COLS):
        cols = slice(c0, c0 + V7X_MXU_COLS)
        y_a = jnp.dot(ya, wpa_ref[:, cols], preferred_element_type=F32)
        y_b = jnp.dot(yb, wpb_ref[:, cols], preferred_element_type=F32)
        mg_ref[:, cols] = (jax.nn.sigmoid(gta_ref[:, cols].astype(F32) + bga_ref[:, cols]) * y_a
                           + jax.nn.sigmoid(gtb_ref[:, cols].astype(F32) + bgb_ref[:, cols]) * y_b).astype(BF16)
    mix = jnp.dot(mg_ref[...], wo_ref[...], preferred_element_type=F32)
    out = _layer_norm(alpha * h_ref[...] + mix, g_ref[...], b_ref[...])
    o_ref[...] = out
    if with_slabs:
        hs_ref = rest[0]
        for g in range(SLAB):
            hs_ref[pl.ds(g, out.shape[0], stride=SLAB), :] = out[:, g * V7X_LANES:(g + 1) * V7X_LANES]


def _mix_out(ya, yb, proj, h, wpa, wpb, wo, bga, bgb, g, b, alpha, with_slabs):
    m, d = h.shape
    tm = _divisor_tile(m, 1056, V7X_BF16_SUBLANES)
    rows = lambda: pl.BlockSpec((tm, d), lambda i: (i, 0))
    out_specs = [rows()]
    out_shape = [jax.ShapeDtypeStruct((m, d), F32)]
    if with_slabs:
        assert d == SLAB * V7X_LANES
        out_specs.append(pl.BlockSpec((tm * SLAB, V7X_LANES), lambda i: (i, 0)))
        out_shape.append(jax.ShapeDtypeStruct((m * SLAB, V7X_LANES), F32))
    return pl.pallas_call(
        functools.partial(_mix_out_body, alpha=alpha, with_slabs=with_slabs),
        grid=(m // tm,),
        in_specs=[rows(), rows(),
                  pl.BlockSpec((tm, d), lambda i: (i, COL_GATE_A // d)),
                  pl.BlockSpec((tm, d), lambda i: (i, COL_GATE_B // d)),
                  rows(), _resident((d, d)), _resident((d, d)), _resident((d, d)),
                  _resident((1, d)), _resident((1, d)), _resident((1, d)), _resident((1, d))],
        out_specs=out_specs,
        out_shape=out_shape,
        scratch_shapes=[pltpu.VMEM((tm, d), BF16)],
        compiler_params=_params("arbitrary"),
        name="mix_out",
    )(ya, yb, proj, proj, h, wpa, wpb, wo, bga, bgb, g, b)


def _swiglu_hidden(xb, wg_ref, wu_ref, hid_ref, widx=()):
    dff = hid_ref.shape[1]
    for c0 in range(0, dff, V7X_MXU_COLS):
        cols = slice(c0, c0 + V7X_MXU_COLS)
        gate = jnp.dot(xb, wg_ref[widx + (slice(None), cols)], preferred_element_type=F32)
        up = jnp.dot(xb, wu_ref[widx + (slice(None), cols)], preferred_element_type=F32)
        hid_ref[:, cols] = (_silu(gate) * up).astype(BF16)


def _ffn_body(x_ref, wg_ref, wu_ref, wd_ref, g_ref, b_ref, o_ref, hid_ref, *, alpha):
    x = x_ref[...]
    _swiglu_hidden(x.astype(BF16), wg_ref, wu_ref, hid_ref)
    f = jnp.dot(hid_ref[...], wd_ref[...], preferred_element_type=F32)
    o_ref[...] = _layer_norm(alpha * x + f, g_ref[...], b_ref[...])


def _ffn(h, wg, wu, wd, g, b, alpha):
    m, d = h.shape
    dff = wg.shape[1]
    assert dff % V7X_MXU_COLS == 0
    tm = _divisor_tile(m, 1056, V7X_BF16_SUBLANES)
    return pl.pallas_call(
        functools.partial(_ffn_body, alpha=alpha),
        grid=(m // tm,),
        in_specs=[pl.BlockSpec((tm, d), lambda i: (i, 0)),
                  _resident((d, dff)), _resident((d, dff)), _resident((dff, d)),
                  _resident((1, d)), _resident((1, d))],
        out_specs=pl.BlockSpec((tm, d), lambda i: (i, 0)),
        out_shape=jax.ShapeDtypeStruct((m, d), F32),
        scratch_shapes=[pltpu.VMEM((tm, dff), BF16)],
        compiler_params=_params("arbitrary"),
        name="ffn",
    )(h, wg, wu, wd, g, b)


def _router_body(h_ref, wr_ref, idx_ref, wt_ref):
    ne = N_EXPERTS
    h = h_ref[...]
    h_hi = h.astype(BF16)
    h_lo = (h - h_hi.astype(F32)).astype(BF16)
    w2 = wr_ref[...]
    both = jnp.dot(h_hi, w2, preferred_element_type=F32)
    logits = both[:, :ne] + both[:, ne:] + jnp.dot(h_lo, w2[:, :ne], preferred_element_type=F32)
    e = lax.broadcasted_iota(jnp.int32, logits.shape, 1)
    m1 = jnp.max(logits, axis=1, keepdims=True)
    i1 = jnp.min(jnp.where(logits == m1, e, ne), axis=1, keepdims=True)
    rest = jnp.where(e == i1, -jnp.inf, logits)
    m2 = jnp.max(rest, axis=1, keepdims=True)
    i2 = jnp.min(jnp.where(rest == m2, e, ne), axis=1, keepdims=True)
    e2 = jnp.exp(m2 - m1)
    w1 = 1.0 / (1.0 + e2)
    idx_ref[...] = jnp.concatenate([i1, i2], axis=1)
    wt_ref[...] = jnp.concatenate([w1, e2 * w1], axis=1)


def _router(h, wr):
    m, d = h.shape
    tm = _divisor_tile(m, 1024, 8)
    w_hi = wr.astype(BF16)
    w2 = jnp.concatenate([w_hi, (wr - w_hi.astype(F32)).astype(BF16)], axis=1)
    return pl.pallas_call(
        _router_body,
        grid=(m // tm,),
        in_specs=[pl.BlockSpec((tm, d), lambda i: (i, 0)), _resident((d, 2 * N_EXPERTS))],
        out_specs=[pl.BlockSpec((tm, 2), lambda i: (i, 0)), pl.BlockSpec((tm, 2), lambda i: (i, 0))],
        out_shape=[jax.ShapeDtypeStruct((m, 2), jnp.int32), jax.ShapeDtypeStruct((m, 2), F32)],
        compiler_params=_params("arbitrary"),
        name="router",
    )(h, w2)


def _index_prefetch(idx_hbm, idx_smem, isem, i, n):
    width = idx_hbm.shape[1]
    slot = lax.rem(i, 2)
    half = lambda sl: idx_smem.at[pl.ds(pl.multiple_of(sl * width, width), width)]

    @pl.when(i == 0)
    def _():
        pltpu.make_async_copy(idx_hbm.at[0], half(0), isem.at[0]).start()

    pltpu.make_async_copy(idx_hbm.at[i], half(slot), isem.at[slot]).wait()

    @pl.when(i + 1 < n)
    def _():
        pltpu.make_async_copy(idx_hbm.at[i + 1], half(1 - slot), isem.at[1 - slot]).start()

    return slot * width


def _slab_at(off):
    return pl.ds(pl.multiple_of(off, SLAB), SLAB)


def _slab_rows(p):
    return pl.ds(pl.multiple_of(p * SLAB, SLAB), SLAB)


def _slabs_to_matrix(ref2, rows):
    return jnp.concatenate([ref2[pl.ds(g, rows, stride=SLAB), :] for g in range(SLAB)], axis=1)


def _dispatch_body(st_ref, cnt_ref, pd_ref, pos_hbm, h3_ref, xs_hbm, idx_smem, zbuf, isem, sem, psem,
                   *, tm, n_rows):
    i = pl.program_id(0)
    n = pl.num_programs(0)
    base = _index_prefetch(pos_hbm, idx_smem, isem, i, n)

    def issue(r, carry):
        src = h3_ref.at[_slab_rows(r)]
        pltpu.make_async_copy(src, xs_hbm.at[_slab_at(idx_smem[base + r])], sem).start()
        pltpu.make_async_copy(src, xs_hbm.at[_slab_at(idx_smem[base + tm + r])], sem).start(priority=1)
        return carry

    lax.fori_loop(0, tm, issue, 0, unroll=8)
    pltpu.make_async_copy(h3_ref, xs_hbm.at[pl.ds(0, tm * SLAB)], sem).wait()
    pltpu.make_async_copy(h3_ref, xs_hbm.at[pl.ds(0, tm * SLAB)], sem).wait()

    @pl.when(i == n - 1)
    def _():
        zbuf[...] = jnp.zeros_like(zbuf)
        zrow = zbuf.at[pl.ds(0, SLAB)]
        for e in range(N_EXPERTS):
            lo = st_ref[e] + cnt_ref[e]
            npad = pd_ref[e] - cnt_ref[e]

            def fill(r, carry):
                pltpu.make_async_copy(zrow, xs_hbm.at[_slab_rows(lo + r)], psem).start()
                return carry

            def drain(r, carry):
                pltpu.make_async_copy(zrow, xs_hbm.at[_slab_rows(lo + r)], psem).wait()
                return carry

            lax.fori_loop(0, npad, fill, 0)
            lax.fori_loop(0, npad, drain, 0)

        zr = zbuf.shape[0] // SLAB
        tail = st_ref[N_EXPERTS - 1] + pd_ref[N_EXPERTS - 1]
        nblk = (n_rows - tail) // zr

        def block(k):
            return xs_hbm.at[pl.ds(pl.multiple_of((tail + k * zr) * SLAB, SLAB), zr * SLAB)]

        def fill_block(k, carry):
            pltpu.make_async_copy(zbuf, block(k), psem).start()
            return carry

        def drain_block(k, carry):
            pltpu.make_async_copy(zbuf, block(k), psem).wait()
            return carry

        lax.fori_loop(0, nblk, fill_block, 0)
        lax.fori_loop(0, nblk, drain_block, 0)


def _dispatch(h3, pos2d, starts, counts, padded, n_rows, tm):
    m = h3.shape[0] // SLAB
    grid_spec = pltpu.PrefetchScalarGridSpec(
        num_scalar_prefetch=3,
        grid=(m // tm,),
        in_specs=[pl.BlockSpec(memory_space=pl.ANY),
                  pl.BlockSpec((tm * SLAB, V7X_LANES), lambda i, *_: (i, 0))],
        out_specs=pl.BlockSpec(memory_space=pl.ANY),
        scratch_shapes=[pltpu.SMEM((4 * tm,), jnp.int32), pltpu.VMEM((ZERO_ROWS * SLAB, V7X_LANES), F32),
                        pltpu.SemaphoreType.DMA((2,)), pltpu.SemaphoreType.DMA, pltpu.SemaphoreType.DMA],
    )
    return pl.pallas_call(
        functools.partial(_dispatch_body, tm=tm, n_rows=n_rows),
        grid_spec=grid_spec,
        out_shape=jax.ShapeDtypeStruct((n_rows * SLAB, V7X_LANES), F32),
        compiler_params=_params("arbitrary"),
        name="moe_dispatch",
    )(starts, counts, padded, pos2d, h3)


def _moe_body(te_ref, nu_ref, x_ref, wg_ref, wu_ref, wd_ref, o_ref, hid_ref):
    rows = hid_ref.shape[0]

    @pl.when(pl.program_id(0) < nu_ref[0])
    def _():
        xb = _slabs_to_matrix(x_ref, rows).astype(BF16)
        _swiglu_hidden(xb, wg_ref, wu_ref, hid_ref, widx=(0,))
        y = jnp.dot(hid_ref[...], wd_ref[0], preferred_element_type=F32)
        for g in range(SLAB):
            o_ref[pl.ds(g, rows, stride=SLAB), :] = y[:, g * V7X_LANES:(g + 1) * V7X_LANES]

    @pl.when(pl.program_id(0) >= nu_ref[0])
    def _():
        o_ref[...] = jnp.zeros_like(o_ref)


def _moe_experts(xs, tile_expert, n_used, wg, wu, wd, tme):
    r = xs.shape[0] // SLAB
    d, dff = wg.shape[1], wg.shape[2]
    assert dff % V7X_MXU_COLS == 0
    expert = lambda shape: pl.BlockSpec((1,) + shape, lambda i, te, nu: (te[i], 0, 0), pipeline_mode=pl.Buffered(1))
    grid_spec = pltpu.PrefetchScalarGridSpec(
        num_scalar_prefetch=2,
        grid=(r // tme,),
        in_specs=[pl.BlockSpec((tme * SLAB, V7X_LANES), lambda i, te, nu: (i, 0)),
                  expert((d, dff)), expert((d, dff)), expert((dff, d))],
        out_specs=pl.BlockSpec((tme * SLAB, V7X_LANES), lambda i, te, nu: (i, 0)),
        scratch_shapes=[pltpu.VMEM((tme, dff), BF16)],
    )
    return pl.pallas_call(
        _moe_body,
        grid_spec=grid_spec,
        out_shape=jax.ShapeDtypeStruct(xs.shape, F32),
        compiler_params=_params("arbitrary"),
        name="moe_experts",
    )(tile_expert, n_used, xs, wg, wu, wd)


def _combine_body(pos_hbm, ys_hbm, wt_ref, h_ref, g_ref, b_ref, o_ref, ybuf, idx_smem, isem, sem, *, tm, alpha):
    i = pl.program_id(0)
    base = _index_prefetch(pos_hbm, idx_smem, isem, i, pl.num_programs(0))

    def issue(r, carry):
        pltpu.make_async_copy(ys_hbm.at[_slab_at(idx_smem[base + r])], ybuf.at[0, _slab_rows(r)], sem).start()
        pltpu.make_async_copy(ys_hbm.at[_slab_at(idx_smem[base + tm + r])], ybuf.at[1, _slab_rows(r)],
                              sem).start(priority=1)
        return carry

    lax.fori_loop(0, tm, issue, 0, unroll=8)
    pltpu.make_async_copy(ys_hbm.at[pl.ds(0, tm * SLAB)], ybuf.at[0], sem).wait()
    pltpu.make_async_copy(ys_hbm.at[pl.ds(0, tm * SLAB)], ybuf.at[1], sem).wait()
    wt = wt_ref[...]
    out = wt[:, 0:1] * _slabs_to_matrix(ybuf.at[0], tm) + wt[:, 1:2] * _slabs_to_matrix(ybuf.at[1], tm)
    o_ref[...] = _layer_norm(alpha * h_ref[...] + out, g_ref[...], b_ref[...])


def _combine(pos2d, ys, wt, h, g, b, alpha, tm):
    m, d = h.shape
    return pl.pallas_call(
        functools.partial(_combine_body, tm=tm, alpha=alpha),
        grid=(m // tm,),
        in_specs=[pl.BlockSpec(memory_space=pl.ANY), pl.BlockSpec(memory_space=pl.ANY),
                  pl.BlockSpec((tm, 2), lambda i: (i, 0)),
                  pl.BlockSpec((tm, d), lambda i: (i, 0)),
                  pl.BlockSpec((1, d), lambda i: (0, 0)), pl.BlockSpec((1, d), lambda i: (0, 0))],
        out_specs=pl.BlockSpec((tm, d), lambda i: (i, 0)),
        out_shape=jax.ShapeDtypeStruct((m, d), F32),
        scratch_shapes=[pltpu.VMEM((2, tm * SLAB, V7X_LANES), F32), pltpu.SMEM((4 * tm,), jnp.int32),
                        pltpu.SemaphoreType.DMA((2,)), pltpu.SemaphoreType.DMA],
        compiler_params=_params("arbitrary"),
        name="moe_combine",
    )(pos2d, ys, wt, h, g, b)


def _moe_layer(h, h_slabs, wr, wg, wu, wd, g, b, alpha):
    m, d = h.shape
    assert d == SLAB * V7X_LANES
    tme = _divisor_tile(m, 1024, ZERO_ROWS)
    tmc = _divisor_tile(m, 1024, 8)
    idx, wts = _router(h, wr)

    e_flat = idx.T.reshape(-1)
    onehot = (e_flat[:, None] == jnp.arange(N_EXPERTS, dtype=jnp.int32)[None, :]).astype(jnp.int32)
    csum = jnp.cumsum(onehot, axis=0)
    rank = jnp.take_along_axis(csum, e_flat[:, None], axis=1)[:, 0] - 1
    counts = csum[-1]
    padded = ((counts + tme - 1) // tme) * tme
    ends = jnp.cumsum(padded)
    starts = ends - padded
    pos = starts[e_flat] + rank
    n_rows = 2 * m + N_EXPERTS * tme
    n_tiles = n_rows // tme
    tile_start = jnp.arange(n_tiles, dtype=jnp.int32) * tme
    tile_expert = jnp.minimum(jnp.sum(tile_start[:, None] >= ends[None, :], axis=1), N_EXPERTS - 1).astype(jnp.int32)
    n_used = (ends[-1] // tme).astype(jnp.int32).reshape(1)
    pos2d = (pos * SLAB).reshape(2, m // tmc, tmc).transpose(1, 0, 2).reshape(m // tmc, 2 * tmc)

    xs = _dispatch(h_slabs, pos2d, starts, counts, padded, n_rows, tmc)
    ys = _moe_experts(xs, tile_expert, n_used, wg, wu, wd, tme)
    return _combine(pos2d, ys, wts, h, g, b, alpha, tmc)


def kernel(x, meta, w_in, b_gate, gla_w_a2, gla_b_a, gla_norm_g, ret_norm_g, ret_norm_b, w_pa, w_pb, w_o,
           ln1_g, ln1_b, ffn_w_gate, ffn_w_up, ffn_w_down, moe_w_router, moe_w_gate, moe_w_up, moe_w_down,
           ln2_g, ln2_b):
    bsz, seq, d = x.shape
    depth = w_in.shape[0]
    alpha = (2 * depth) ** 0.25
    lp = seq + N_META + PAD
    assert lp % CHUNK == 0 and d == VW
    m = bsz * lp

    h = jnp.concatenate([jnp.zeros((bsz, PAD, d), x.dtype),
                         jnp.broadcast_to(meta[None].astype(x.dtype), (bsz, N_META, d)), x], axis=1)
    h = h.reshape(m, d)
    tables = _retention_tables(lp)
    keep = jnp.asarray(np.tile((np.arange(lp) >= PAD).astype(np.float32), bsz).reshape(m, 1))

    c0 = 2 * QK + 2 * VW
    row = lambda a: a.reshape(1, -1)
    for l in range(depth):
        w = w_in[l]
        w_perm = jnp.concatenate(
            [w[:, :c0], w[:, c0 + GLA_RANK:], w[:, c0:c0 + GLA_RANK],
             jnp.zeros((d, V7X_MXU_COLS - GLA_RANK), w.dtype)], axis=1).astype(BF16)
        wa2 = jnp.concatenate([gla_w_a2[l], jnp.zeros((V7X_LANES - GLA_RANK, QK), F32)], axis=0).astype(BF16)

        proj = _in_proj(h, keep, w_perm)
        ya, yb = _mixer(proj.reshape(bsz, lp, N_PROJ), wa2, row(gla_b_a[l]), row(gla_norm_g[l]),
                        row(ret_norm_g[l]), row(ret_norm_b[l]), tables)
        is_moe = l % 2 == 1
        mixed = _mix_out(ya.reshape(m, VW), yb.reshape(m, VW), proj, h,
                         w_pa[l].astype(BF16), w_pb[l].astype(BF16), w_o[l].astype(BF16),
                         row(b_gate[l, 0]), row(b_gate[l, 1]), row(ln1_g[l]), row(ln1_b[l]), alpha, is_moe)
        h = mixed[0]
        if l % 2 == 0:
            j = l // 2
            h = _ffn(h, ffn_w_gate[j].astype(BF16), ffn_w_up[j].astype(BF16), ffn_w_down[j].astype(BF16),
                     row(ln2_g[l]), row(ln2_b[l]), alpha)
        else:
            j = l // 2
            h = _moe_layer(h, mixed[1], moe_w_router[j], moe_w_gate[j].astype(BF16), moe_w_up[j].astype(BF16),
                           moe_w_down[j].astype(BF16), row(ln2_g[l]), row(ln2_b[l]), alpha)
    return h.reshape(bsz, lp, d)[:, PAD + N_META:]
```

```python
import functools
import math

import jax
import jax.numpy as jnp
import numpy as np
from jax import lax
from jax.experimental import pallas as pl
from jax.experimental.pallas import tpu as pltpu

F32 = jnp.float32
BF16 = jnp.bfloat16

N_META = 16
CHUNK = 64
PAD = CHUNK - N_META
HEADS = 4
HEADS_PER_STEP = 2
CHUNKS_PER_TRIP = 3
DK = 128
DV = 256
GLA_RANK = 16
GLA_TAU = 16.0
ROPE_BASE = 10000.0
N_EXPERTS = 8
LN_EPS = 1e-5
GN_EPS = 1e-6

V7X_LANES = 128
V7X_BF16_SUBLANES = 16
V7X_MXU_COLS = 256
V7X_VMEM_LIMIT_BYTES = 56 * 1024 * 1024

SLAB = 8
ZERO_ROWS = 128

QK = HEADS * DK
VW = HEADS * DV
COL_GQ, COL_GK, COL_GV, COL_GR = 0, QK, 2 * QK, 2 * QK + VW
COL_RQ = 2 * QK + 2 * VW
COL_RK, COL_RV, COL_RG = COL_RQ + QK, COL_RQ + 2 * QK, COL_RQ + 2 * QK + VW
COL_GATE_A = COL_RQ + 2 * QK + 2 * VW
COL_GATE_B = COL_GATE_A + QK + QK
COL_GA = COL_GATE_B + QK + QK
N_PROJ = COL_GA + V7X_MXU_COLS


def _divisor_tile(n, cap, multiple):
    best = None
    for t in range(multiple, min(n, cap) + 1, multiple):
        if n % t == 0:
            best = t
    assert best is not None, (n, cap, multiple)
    return best


def _params(*sem):
    return pltpu.CompilerParams(dimension_semantics=sem, vmem_limit_bytes=V7X_VMEM_LIMIT_BYTES)


def _resident(shape):
    return pl.BlockSpec(shape, lambda *_: (0,) * len(shape), pipeline_mode=pl.Buffered(1))


def _layer_norm(r, g, b):
    mu = jnp.mean(r, axis=-1, keepdims=True)
    d = r - mu
    var = jnp.mean(d * d, axis=-1, keepdims=True)
    return d * lax.rsqrt(var + LN_EPS) * g + b


def _silu(x):
    return x * jax.nn.sigmoid(x)


def _in_proj_body(x_ref, keep_ref, w_ref, o_ref, xb_ref):
    @pl.when(pl.program_id(1) == 0)
    def _():
        xb_ref[...] = jnp.where(keep_ref[...] > 0.0, x_ref[...], 0.0).astype(BF16)

    xb = xb_ref[...]
    for c0 in range(0, o_ref.shape[1], V7X_MXU_COLS):
        cols = slice(c0, c0 + V7X_MXU_COLS)
        o_ref[:, cols] = jnp.dot(xb, w_ref[:, cols], preferred_element_type=F32).astype(BF16)


def _in_proj(h, keep, w):
    m, d = h.shape
    n = w.shape[1]
    tm = _divisor_tile(m, 1056, V7X_BF16_SUBLANES)
    tn = _divisor_tile(n, 2816, V7X_MXU_COLS)
    return pl.pallas_call(
        _in_proj_body,
        grid=(m // tm, n // tn),
        in_specs=[pl.BlockSpec((tm, d), lambda i, j: (i, 0)), pl.BlockSpec((tm, 1), lambda i, j: (i, 0)),
                  pl.BlockSpec((d, tn), lambda i, j: (0, j))],
        out_specs=pl.BlockSpec((tm, tn), lambda i, j: (i, j)),
        out_shape=jax.ShapeDtypeStruct((m, n), BF16),
        scratch_shapes=[pltpu.VMEM((tm, d), BF16)],
        compiler_params=_params("arbitrary", "arbitrary"),
        name="in_proj",
    )(h, keep, w)


def _mixer_body(gq_ref, gk_ref, gv_ref, gr_ref, ga_ref, rq_ref, rk_ref, rv_ref, rg_ref,
                wa2_ref, ba_ref, gng_ref, rng_ref, rnb_ref, cos_ref, sin_ref,
                dmat_ref, xi_ref, zeta_ref, rdec_ref,
                ya_ref, yb_ref, sa_ref, sb_ref, b_ref, oa_ref, ob_ref, *, n_groups, gsz, hps):
    c_ = CHUNK
    r_ = gsz * c_
    heads = range(hps)
    sa_ref[...] = jnp.zeros_like(sa_ref)
    sb_ref[...] = jnp.zeros_like(sb_ref)
    row = lax.broadcasted_iota(jnp.int32, (c_, c_), 0)
    col = lax.broadcasted_iota(jnp.int32, (c_, c_), 1)
    causal = row >= col
    grow = lax.broadcasted_iota(jnp.int32, (r_, r_), 0)
    gcol = lax.broadcasted_iota(jnp.int32, (r_, r_), 1)
    shift = int(math.log2(c_))
    gtri = jnp.where((jnp.right_shift(grow, shift) == jnp.right_shift(gcol, shift)) & (grow >= gcol),
                     1.0, 0.0).astype(BF16)
    wa2 = wa2_ref[...]
    ba = ba_ref[...]
    gng = gng_ref[...]
    rng = rng_ref[...]
    rnb = rnb_ref[...]
    nt = (((1,), (1,)), ((), ()))
    tn = (((0,), (0,)), ((), ()))
    mm = functools.partial(jnp.dot, preferred_element_type=F32)
    mmg = functools.partial(lax.dot_general, preferred_element_type=F32)
    chunks = [slice(g * c_, (g + 1) * c_) for g in range(gsz)]
    hk = lambda x, hh: x[:, hh * DK:(hh + 1) * DK]
    hv = lambda x, hh: x[:, hh * DV:(hh + 1) * DV]
    per_head = lambda xs: jnp.concatenate(xs, axis=1)

    def group_rows(j):
        return pl.ds(pl.multiple_of(j * r_, c_), r_)

    def gate_logits(j):
        return mm(ga_ref[0, group_rows(j), :], wa2) + ba

    def log_decay(z, j):
        valid = j * r_ + lax.broadcasted_iota(jnp.int32, (r_, 1), 0) >= PAD
        log_sig = jnp.minimum(z, 0.0) - jnp.log(1.0 + jnp.exp(-jnp.abs(z)))
        la = jnp.where(valid, log_sig * (1.0 / GLA_TAU), 0.0)
        la_hi = la.astype(BF16)
        la_lo = (la - la_hi.astype(F32)).astype(BF16)
        return mm(gtri, la_hi) + mm(gtri, la_lo)

    def finish(j, slot):
        rows = group_rows(j)
        o = oa_ref[slot]
        on = []
        for hh in heads:
            o_h = hv(o, hh)
            ms = jnp.mean(o_h * o_h, axis=-1, keepdims=True)
            on.append(o_h * lax.rsqrt(ms + LN_EPS) * gng)
        ya_ref[0, rows, :] = _silu(gr_ref[0, rows, :]) * per_head(on).astype(BF16)
        ob = ob_ref[slot]
        obn = []
        for hh in heads:
            ob_h = hv(ob, hh)
            dd = ob_h - jnp.mean(ob_h, axis=-1, keepdims=True)
            var = jnp.mean(dd * dd, axis=-1, keepdims=True)
            obn.append(dd * lax.rsqrt(var + GN_EPS))
        yb_ref[0, rows, :] = _silu(rg_ref[0, rows, :]) * (per_head(obn) * rng + rnb).astype(BF16)

    oa_ref[...] = jnp.zeros_like(oa_ref)
    ob_ref[...] = jnp.zeros_like(ob_ref)
    b_ref[0] = log_decay(gate_logits(0), 0)

    def group(gi, carry):
        slot = lax.rem(gi, 2)
        nxt = jnp.minimum(gi + 1, n_groups - 1)
        z_next = gate_logits(nxt)

        rows = group_rows(gi)
        cos = cos_ref[rows, :]
        sin = sin_ref[rows, :]
        rq = rq_ref[0, rows, :].astype(F32)
        rk = rk_ref[0, rows, :].astype(F32)
        vb = rv_ref[0, rows, :]
        qr, s_b, st_b = [], [], []
        for hh in heads:
            rq_h, rk_h = hk(rq, hh), hk(rk, hh)
            qr.append(rq_h * cos + pltpu.roll(rq_h, DK // 2, 1) * sin)
            kr = (rk_h * cos + pltpu.roll(rk_h, DK // 2, 1) * sin) * (DK ** -0.5)
            qr_b, kr_b = qr[hh].astype(BF16), kr.astype(BF16)
            s_b.append([(mmg(qr_b[cs], kr_b[cs], nt) * dmat_ref[hh]).astype(BF16) for cs in chunks])
            st, prev = sb_ref[hh], []
            for cs in chunks:
                prev.append(st.astype(BF16))
                st = st * rdec_ref[hh] + mmg((kr[cs] * zeta_ref[hh]).astype(BF16), hv(vb, hh)[cs], tn)
            sb_ref[hh] = st
            st_b.append(prev)

        b = b_ref[slot]
        q = gq_ref[0, rows, :].astype(F32) * (DK ** -0.5)
        k = gk_ref[0, rows, :].astype(F32)
        v = gv_ref[0, rows, :]
        q_in = (q * jnp.exp(b)).astype(BF16)
        k_in = (k * jnp.exp(-b)).astype(BF16)
        s_a, st_a = [], []
        for hh in heads:
            s_a.append([jnp.where(causal, mmg(hk(q_in, hh)[cs], hk(k_in, hh)[cs], nt), 0.0).astype(BF16)
                        for cs in chunks])
            st, prev = sa_ref[hh], []
            for cs in chunks:
                b_c = hk(b, hh)[cs]
                b_last = b_c[c_ - 1:c_, :]
                k_tail = (hk(k, hh)[cs] * jnp.exp(b_last - b_c)).astype(BF16)
                dcol = jnp.transpose(jnp.broadcast_to(jnp.exp(b_last), (DK, DK)))
                prev.append(st.astype(BF16))
                st = st * jnp.concatenate([dcol, dcol], axis=1) + mmg(k_tail, hv(v, hh)[cs], tn)
            sa_ref[hh] = st
            st_a.append(prev)

        b_ref[1 - slot] = log_decay(z_next, nxt)
        finish(jnp.maximum(gi - 1, 0), 1 - slot)

        o_a, o_b = [], []
        for hh in heads:
            o_b.append(jnp.concatenate(
                [mm(s_b[hh][g], hv(vb, hh)[cs]) + mm((qr[hh][cs] * xi_ref[hh]).astype(BF16), st_b[hh][g])
                 for g, cs in enumerate(chunks)], axis=0))
        for hh in heads:
            o_a.append(jnp.concatenate(
                [mm(s_a[hh][g], hv(v, hh)[cs]) + mm(hk(q_in, hh)[cs], st_a[hh][g])
                 for g, cs in enumerate(chunks)], axis=0))
        oa_ref[slot] = per_head(o_a)
        ob_ref[slot] = per_head(o_b)
        return carry

    lax.fori_loop(0, n_groups, group, 0)
    finish(n_groups - 1, (n_groups - 1) % 2)


def _retention_tables(lp):
    half = DK // 2
    pos = np.arange(lp, dtype=np.float64) - PAD
    inv = ROPE_BASE ** (-np.arange(half, dtype=np.float64) / half)
    ang = pos[:, None] * inv[None, :]
    cos = np.concatenate([np.cos(ang), np.cos(ang)], axis=1)
    sin = np.concatenate([-np.sin(ang), np.sin(ang)], axis=1)
    lg = np.log(1.0 - 2.0 ** (-5.0 - np.arange(HEADS, dtype=np.float64)))
    idx = np.arange(CHUNK, dtype=np.float64)
    rel = idx[:, None] - idx[None, :]
    dmat = np.where(rel >= 0, np.exp(lg[:, None, None] * np.maximum(rel, 0.0)), 0.0)
    xi = np.broadcast_to(np.exp(lg[:, None] * (idx + 1.0))[:, :, None], (HEADS, CHUNK, DK))
    zeta = np.broadcast_to(np.exp(lg[:, None] * (CHUNK - 1.0 - idx))[:, :, None], (HEADS, CHUNK, DK))
    rdec = np.broadcast_to(np.exp(lg * CHUNK)[:, None, None], (HEADS, 1, DV))
    f = lambda a: jnp.asarray(np.ascontiguousarray(a), F32)
    return f(cos), f(sin), f(dmat), f(xi), f(zeta), f(rdec)


def _mixer(proj3, wa2, ba, gng, rng, rnb, tables):
    bsz, lp, _ = proj3.shape
    cos, sin, dmat, xi, zeta, rdec = tables
    n_chunks = lp // CHUNK
    gsz = CHUNKS_PER_TRIP if n_chunks % CHUNKS_PER_TRIP == 0 else 1
    hps = HEADS_PER_STEP
    kw, vw = hps * DK, hps * DV
    qk = lambda col: pl.BlockSpec((1, lp, kw), lambda b, h, c=col // kw: (b, 0, c + h))
    vv = lambda col: pl.BlockSpec((1, lp, vw), lambda b, h, c=col // vw: (b, 0, c + h))
    in_specs = [
        qk(COL_GQ), qk(COL_GK), vv(COL_GV), vv(COL_GR),
        pl.BlockSpec((1, lp, V7X_LANES), lambda b, h: (b, 0, COL_GA // V7X_LANES)),
        qk(COL_RQ), qk(COL_RK), vv(COL_RV), vv(COL_RG),
        pl.BlockSpec((V7X_LANES, kw), lambda b, h: (0, h)),
        pl.BlockSpec((1, kw), lambda b, h: (0, h)),
        pl.BlockSpec((1, DV), lambda b, h: (0, 0)),
        pl.BlockSpec((1, vw), lambda b, h: (0, h)),
        pl.BlockSpec((1, vw), lambda b, h: (0, h)),
        pl.BlockSpec((lp, DK), lambda b, h: (0, 0)),
        pl.BlockSpec((lp, DK), lambda b, h: (0, 0)),
        pl.BlockSpec((hps, CHUNK, CHUNK), lambda b, h: (h, 0, 0)),
        pl.BlockSpec((hps, CHUNK, DK), lambda b, h: (h, 0, 0)),
        pl.BlockSpec((hps, CHUNK, DK), lambda b, h: (h, 0, 0)),
        pl.BlockSpec((hps, 1, DV), lambda b, h: (h, 0, 0)),
    ]
    out_spec = pl.BlockSpec((1, lp, vw), lambda b, h: (b, 0, h))
    return pl.pallas_call(
        functools.partial(_mixer_body, n_groups=n_chunks // gsz, gsz=gsz, hps=hps),
        grid=(bsz, HEADS // hps),
        in_specs=in_specs,
        out_specs=[out_spec, out_spec],
        out_shape=[jax.ShapeDtypeStruct((bsz, lp, VW), BF16)] * 2,
        scratch_shapes=[pltpu.VMEM((hps, DK, DV), F32), pltpu.VMEM((hps, DK, DV), F32),
                        pltpu.VMEM((2, gsz * CHUNK, kw), F32),
                        pltpu.VMEM((2, gsz * CHUNK, vw), F32), pltpu.VMEM((2, gsz * CHUNK, vw), F32)],
        compiler_params=_params("arbitrary", "arbitrary"),
        name="mixer",
    )(*([proj3] * 4), proj3, *([proj3] * 4), wa2, ba, gng, rng, rnb, cos, sin, dmat, xi, zeta, rdec)


def _mix_out_body(ya_ref, yb_ref, gta_ref, gtb_ref, h_ref, wpa_ref, wpb_ref, wo_ref,
                  bga_ref, bgb_ref, g_ref, b_ref, o_ref, *rest, alpha, with_slabs):
    mg_ref = rest[-1]
    ya = ya_ref[...]
    yb = yb_ref[...]
    for c0 in range(0, mg_ref.shape[1], V7X_MXU_COLS):
        cols = slice(c0, c0 + V7X_MXU_COLS)
        y_a = jnp.dot(ya, wpa_ref[:, cols], preferred_element_type=F32)
        y_b = jnp.dot(yb, wpb_ref[:, cols], preferred_element_type=F32)
        mg_ref[:, cols] = (jax.nn.sigmoid(gta_ref[:, cols].astype(F32) + bga_ref[:, cols]) * y_a
                           + jax.nn.sigmoid(gtb_ref[:, cols].astype(F32) + bgb_ref[:, cols]) * y_b).astype(BF16)
    mix = jnp.dot(mg_ref[...], wo_ref[...], preferred_element_type=F32)
    out = _layer_norm(alpha * h_ref[...] + mix, g_ref[...], b_ref[...])
    o_ref[...] = out
    if with_slabs:
        _matrix_to_slabs(out, rest[0])


def _mix_out(ya, yb, proj, h, wpa, wpb, wo, bga, bgb, g, b, alpha, with_slabs):
    m, d = h.shape
    tm = _divisor_tile(m, 1056, V7X_BF16_SUBLANES)
    rows = lambda: pl.BlockSpec((tm, d), lambda i: (i, 0))
    out_specs = [rows()]
    out_shape = [jax.ShapeDtypeStruct((m, d), F32)]
    if with_slabs:
        assert d == SLAB * V7X_LANES
        out_specs.append(pl.BlockSpec((tm * SLAB, V7X_LANES), lambda i: (i, 0)))
        out_shape.append(jax.ShapeDtypeStruct((m * SLAB, V7X_LANES), F32))
    return pl.pallas_call(
        functools.partial(_mix_out_body, alpha=alpha, with_slabs=with_slabs),
        grid=(m // tm,),
        in_specs=[rows(), rows(),
                  pl.BlockSpec((tm, d), lambda i: (i, COL_GATE_A // d)),
                  pl.BlockSpec((tm, d), lambda i: (i, COL_GATE_B // d)),
                  rows(), _resident((d, d)), _resident((d, d)), _resident((d, d)),
                  _resident((1, d)), _resident((1, d)), _resident((1, d)), _resident((1, d))],
        out_specs=out_specs,
        out_shape=out_shape,
        scratch_shapes=[pltpu.VMEM((tm, d), BF16)],
        compiler_params=_params("arbitrary"),
        name="mix_out",
    )(ya, yb, proj, proj, h, wpa, wpb, wo, bga, bgb, g, b)


def _swiglu_hidden(xb, wg_ref, wu_ref, hid_ref, widx=()):
    dff = hid_ref.shape[1]
    for c0 in range(0, dff, V7X_MXU_COLS):
        cols = slice(c0, c0 + V7X_MXU_COLS)
        gate = jnp.dot(xb, wg_ref[widx + (slice(None), cols)], preferred_element_type=F32)
        up = jnp.dot(xb, wu_ref[widx + (slice(None), cols)], preferred_element_type=F32)
        hid_ref[:, cols] = (_silu(gate) * up).astype(BF16)


def _ffn_body(x_ref, wg_ref, wu_ref, wd_ref, g_ref, b_ref, o_ref, hid_ref, *, alpha):
    x = x_ref[...]
    _swiglu_hidden(x.astype(BF16), wg_ref, wu_ref, hid_ref)
    f = jnp.dot(hid_ref[...], wd_ref[...], preferred_element_type=F32)
    o_ref[...] = _layer_norm(alpha * x + f, g_ref[...], b_ref[...])


def _ffn(h, wg, wu, wd, g, b, alpha):
    m, d = h.shape
    dff = wg.shape[1]
    assert dff % V7X_MXU_COLS == 0
    tm = _divisor_tile(m, 1056, V7X_BF16_SUBLANES)
    return pl.pallas_call(
        functools.partial(_ffn_body, alpha=alpha),
        grid=(m // tm,),
        in_specs=[pl.BlockSpec((tm, d), lambda i: (i, 0)),
                  _resident((d, dff)), _resident((d, dff)), _resident((dff, d)),
                  _resident((1, d)), _resident((1, d))],
        out_specs=pl.BlockSpec((tm, d), lambda i: (i, 0)),
        out_shape=jax.ShapeDtypeStruct((m, d), F32),
        scratch_shapes=[pltpu.VMEM((tm, dff), BF16)],
        compiler_params=_params("arbitrary"),
        name="ffn",
    )(h, wg, wu, wd, g, b)


def _router_body(h_ref, wr_ref, idx_ref, wt_ref, rank_ref, cnt_ref, tri_ref, run_ref):
    ne = N_EXPERTS
    tm = h_ref.shape[0]

    @pl.when(pl.program_id(0) == 0)
    def _():
        r = lax.broadcasted_iota(jnp.int32, (tm, tm), 0)
        c = lax.broadcasted_iota(jnp.int32, (tm, tm), 1)
        tri_ref[...] = jnp.where(r > c, 1.0, 0.0).astype(BF16)
        run_ref[...] = jnp.zeros_like(run_ref)

    h = h_ref[...]
    h_hi = h.astype(BF16)
    h_lo = (h - h_hi.astype(F32)).astype(BF16)
    w2 = wr_ref[...]
    both = jnp.dot(h_hi, w2, preferred_element_type=F32)
    logits = both[:, :ne] + both[:, ne:] + jnp.dot(h_lo, w2[:, :ne], preferred_element_type=F32)
    e = lax.broadcasted_iota(jnp.int32, logits.shape, 1)
    m1 = jnp.max(logits, axis=1, keepdims=True)
    i1 = jnp.min(jnp.where(logits == m1, e, ne), axis=1, keepdims=True)
    rest = jnp.where(e == i1, -jnp.inf, logits)
    m2 = jnp.max(rest, axis=1, keepdims=True)
    i2 = jnp.min(jnp.where(rest == m2, e, ne), axis=1, keepdims=True)
    e2 = jnp.exp(m2 - m1)
    w1 = 1.0 / (1.0 + e2)
    idx_ref[...] = jnp.concatenate([i1, i2], axis=1)
    wt_ref[...] = jnp.concatenate([w1, e2 * w1], axis=1)

    one1 = jnp.where(e == i1, 1.0, 0.0)
    one2 = jnp.where(e == i2, 1.0, 0.0)
    both_slots = one1 + one2
    before = jnp.dot(tri_ref[...], both_slots.astype(BF16), preferred_element_type=F32) + run_ref[...]
    r1 = jnp.sum(one1 * before, axis=1, keepdims=True)
    r2 = jnp.sum(one2 * before, axis=1, keepdims=True)
    rank_ref[...] = jnp.concatenate([r1, r2], axis=1).astype(jnp.int32)
    run_ref[...] += jnp.sum(both_slots, axis=0, keepdims=True)
    cnt_ref[...] = run_ref[...].astype(jnp.int32)


def _router(h, wr):
    m, d = h.shape
    tm = _divisor_tile(m, 1024, V7X_BF16_SUBLANES)
    w_hi = wr.astype(BF16)
    w2 = jnp.concatenate([w_hi, (wr - w_hi.astype(F32)).astype(BF16)], axis=1)
    per_token = lambda: pl.BlockSpec((tm, 2), lambda i: (i, 0))
    return pl.pallas_call(
        _router_body,
        grid=(m // tm,),
        in_specs=[pl.BlockSpec((tm, d), lambda i: (i, 0)), _resident((d, 2 * N_EXPERTS))],
        out_specs=[per_token(), per_token(), per_token(), pl.BlockSpec((1, N_EXPERTS), lambda i: (0, 0))],
        out_shape=[jax.ShapeDtypeStruct((m, 2), jnp.int32), jax.ShapeDtypeStruct((m, 2), F32),
                   jax.ShapeDtypeStruct((m, 2), jnp.int32), jax.ShapeDtypeStruct((1, N_EXPERTS), jnp.int32)],
        scratch_shapes=[pltpu.VMEM((tm, tm), BF16), pltpu.VMEM((1, N_EXPERTS), F32)],
        compiler_params=_params("arbitrary"),
        name="router",
    )(h, w2)


def _index_prefetch(idx_hbm, idx_smem, isem, i, n):
    width = idx_hbm.shape[1]
    slot = lax.rem(i, 2)
    half = lambda sl: idx_smem.at[pl.ds(pl.multiple_of(sl * width, width), width)]

    @pl.when(i == 0)
    def _():
        pltpu.make_async_copy(idx_hbm.at[0], half(0), isem.at[0]).start()

    pltpu.make_async_copy(idx_hbm.at[i], half(slot), isem.at[slot]).wait()

    @pl.when(i + 1 < n)
    def _():
        pltpu.make_async_copy(idx_hbm.at[i + 1], half(1 - slot), isem.at[1 - slot]).start()

    return slot * width


def _slab_at(off):
    return pl.ds(pl.multiple_of(off, SLAB), SLAB)


def _slab_rows(p):
    return pl.ds(pl.multiple_of(p * SLAB, SLAB), SLAB)


def _slabs_to_matrix(ref2, rows):
    return jnp.concatenate([ref2[pl.ds(g, rows, stride=SLAB), :] for g in range(SLAB)], axis=1)


def _matrix_to_slabs(x, ref2):
    for g in range(SLAB):
        ref2[pl.ds(g, x.shape[0], stride=SLAB), :] = x[:, g * V7X_LANES:(g + 1) * V7X_LANES]


def _dispatch_body(st_ref, cnt_ref, pd_ref, pos_hbm, h3_ref, xs_hbm, idx_smem, zbuf, isem, sem, psem,
                   *, tm, n_rows):
    i = pl.program_id(0)
    n = pl.num_programs(0)
    base = _index_prefetch(pos_hbm, idx_smem, isem, i, n)

    def issue(r, carry):
        src = h3_ref.at[_slab_rows(r)]
        pltpu.make_async_copy(src, xs_hbm.at[_slab_at(idx_smem[base + r])], sem).start()
        pltpu.make_async_copy(src, xs_hbm.at[_slab_at(idx_smem[base + tm + r])], sem).start(priority=1)
        return carry

    lax.fori_loop(0, tm, issue, 0, unroll=8)
    pltpu.make_async_copy(h3_ref, xs_hbm.at[pl.ds(0, tm * SLAB)], sem).wait()
    pltpu.make_async_copy(h3_ref, xs_hbm.at[pl.ds(0, tm * SLAB)], sem).wait()

    @pl.when(i == n - 1)
    def _():
        zbuf[...] = jnp.zeros_like(zbuf)
        zrow = zbuf.at[pl.ds(0, SLAB)]
        for e in range(N_EXPERTS):
            lo = st_ref[e] + cnt_ref[e]
            npad = pd_ref[e] - cnt_ref[e]

            def fill(r, carry):
                pltpu.make_async_copy(zrow, xs_hbm.at[_slab_rows(lo + r)], psem).start()
                return carry

            def drain(r, carry):
                pltpu.make_async_copy(zrow, xs_hbm.at[_slab_rows(lo + r)], psem).wait()
                return carry

            lax.fori_loop(0, npad, fill, 0)
            lax.fori_loop(0, npad, drain, 0)

        zr = zbuf.shape[0] // SLAB
        tail = st_ref[N_EXPERTS - 1] + pd_ref[N_EXPERTS - 1]
        nblk = (n_rows - tail) // zr

        def block(k):
            return xs_hbm.at[pl.ds(pl.multiple_of((tail + k * zr) * SLAB, SLAB), zr * SLAB)]

        def fill_block(k, carry):
            pltpu.make_async_copy(zbuf, block(k), psem).start()
            return carry

        def drain_block(k, carry):
            pltpu.make_async_copy(zbuf, block(k), psem).wait()
            return carry

        lax.fori_loop(0, nblk, fill_block, 0)
        lax.fori_loop(0, nblk, drain_block, 0)


def _dispatch(h3, pos2d, starts, counts, padded, n_rows, tm):
    m = h3.shape[0] // SLAB
    grid_spec = pltpu.PrefetchScalarGridSpec(
        num_scalar_prefetch=3,
        grid=(m // tm,),
        in_specs=[pl.BlockSpec(memory_space=pl.ANY),
                  pl.BlockSpec((tm * SLAB, V7X_LANES), lambda i, *_: (i, 0))],
        out_specs=pl.BlockSpec(memory_space=pl.ANY),
        scratch_shapes=[pltpu.SMEM((4 * tm,), jnp.int32), pltpu.VMEM((ZERO_ROWS * SLAB, V7X_LANES), F32),
                        pltpu.SemaphoreType.DMA((2,)), pltpu.SemaphoreType.DMA, pltpu.SemaphoreType.DMA],
    )
    return pl.pallas_call(
        functools.partial(_dispatch_body, tm=tm, n_rows=n_rows),
        grid_spec=grid_spec,
        out_shape=jax.ShapeDtypeStruct((n_rows * SLAB, V7X_LANES), F32),
        compiler_params=_params("arbitrary"),
        name="moe_dispatch",
    )(starts, counts, padded, pos2d, h3)


def _moe_body(te_ref, nu_ref, x_ref, wg_ref, wu_ref, wd_ref, o_ref, hid_ref):
    @pl.when(pl.program_id(0) < nu_ref[0])
    def _():
        xb = _slabs_to_matrix(x_ref, hid_ref.shape[0]).astype(BF16)
        _swiglu_hidden(xb, wg_ref, wu_ref, hid_ref, widx=(0,))
        _matrix_to_slabs(jnp.dot(hid_ref[...], wd_ref[0], preferred_element_type=F32), o_ref)

    @pl.when(pl.program_id(0) >= nu_ref[0])
    def _():
        o_ref[...] = jnp.zeros_like(o_ref)


def _moe_experts(xs, tile_expert, n_used, wg, wu, wd, tme):
    r = xs.shape[0] // SLAB
    d, dff = wg.shape[1], wg.shape[2]
    assert dff % V7X_MXU_COLS == 0
    expert = lambda shape: pl.BlockSpec((1,) + shape, lambda i, te, nu: (te[i], 0, 0), pipeline_mode=pl.Buffered(1))
    grid_spec = pltpu.PrefetchScalarGridSpec(
        num_scalar_prefetch=2,
        grid=(r // tme,),
        in_specs=[pl.BlockSpec((tme * SLAB, V7X_LANES), lambda i, te, nu: (i, 0)),
                  expert((d, dff)), expert((d, dff)), expert((dff, d))],
        out_specs=pl.BlockSpec((tme * SLAB, V7X_LANES), lambda i, te, nu: (i, 0)),
        scratch_shapes=[pltpu.VMEM((tme, dff), BF16)],
    )
    return pl.pallas_call(
        _moe_body,
        grid_spec=grid_spec,
        out_shape=jax.ShapeDtypeStruct(xs.shape, F32),
        compiler_params=_params("arbitrary"),
        name="moe_experts",
    )(tile_expert, n_used, xs, wg, wu, wd)


def _combine_body(pos_hbm, ys_hbm, wt_ref, h_ref, g_ref, b_ref, o_ref, ybuf, idx_smem, isem, sem, *, tm, alpha):
    i = pl.program_id(0)
    base = _index_prefetch(pos_hbm, idx_smem, isem, i, pl.num_programs(0))

    def issue(r, carry):
        pltpu.make_async_copy(ys_hbm.at[_slab_at(idx_smem[base + r])], ybuf.at[0, _slab_rows(r)], sem).start()
        pltpu.make_async_copy(ys_hbm.at[_slab_at(idx_smem[base + tm + r])], ybuf.at[1, _slab_rows(r)],
                              sem).start(priority=1)
        return carry

    lax.fori_loop(0, tm, issue, 0, unroll=8)
    pltpu.make_async_copy(ys_hbm.at[pl.ds(0, tm * SLAB)], ybuf.at[0], sem).wait()
    pltpu.make_async_copy(ys_hbm.at[pl.ds(0, tm * SLAB)], ybuf.at[1], sem).wait()
    wt = wt_ref[...]
    out = wt[:, 0:1] * _slabs_to_matrix(ybuf.at[0], tm) + wt[:, 1:2] * _slabs_to_matrix(ybuf.at[1], tm)
    o_ref[...] = _layer_norm(alpha * h_ref[...] + out, g_ref[...], b_ref[...])


def _combine(pos2d, ys, wt, h, g, b, alpha, tm):
    m, d = h.shape
    return pl.pallas_call(
        functools.partial(_combine_body, tm=tm, alpha=alpha),
        grid=(m // tm,),
        in_specs=[pl.BlockSpec(memory_space=pl.ANY), pl.BlockSpec(memory_space=pl.ANY),
                  pl.BlockSpec((tm, 2), lambda i: (i, 0)),
                  pl.BlockSpec((tm, d), lambda i: (i, 0)),
                  pl.BlockSpec((1, d), lambda i: (0, 0)), pl.BlockSpec((1, d), lambda i: (0, 0))],
        out_specs=pl.BlockSpec((tm, d), lambda i: (i, 0)),
        out_shape=jax.ShapeDtypeStruct((m, d), F32),
        scratch_shapes=[pltpu.VMEM((2, tm * SLAB, V7X_LANES), F32), pltpu.SMEM((4 * tm,), jnp.int32),
                        pltpu.SemaphoreType.DMA((2,)), pltpu.SemaphoreType.DMA],
        compiler_params=_params("arbitrary"),
        name="moe_combine",
    )(pos2d, ys, wt, h, g, b)


def _moe_layer(h, h_slabs, wr, wg, wu, wd, g, b, alpha):
    m, d = h.shape
    assert d == SLAB * V7X_LANES
    tme = _divisor_tile(m, 1024, ZERO_ROWS)
    tmc = _divisor_tile(m, 1024, 8)
    idx, wts, rank, counts = _router(h, wr)

    counts = counts[0]
    padded = ((counts + tme - 1) // tme) * tme
    ends = jnp.cumsum(padded)
    starts = ends - padded
    pos = jnp.sum(jnp.where(idx[:, :, None] == jnp.arange(N_EXPERTS, dtype=jnp.int32), starts, 0), axis=-1) + rank
    n_rows = 2 * m + N_EXPERTS * tme
    n_tiles = n_rows // tme
    tile_start = jnp.arange(n_tiles, dtype=jnp.int32) * tme
    tile_expert = jnp.minimum(jnp.sum(tile_start[:, None] >= ends[None, :], axis=1), N_EXPERTS - 1).astype(jnp.int32)
    n_used = (ends[-1] // tme).astype(jnp.int32).reshape(1)
    pos2d = (pos * SLAB).reshape(m // tmc, tmc, 2).transpose(0, 2, 1).reshape(m // tmc, 2 * tmc)

    xs = _dispatch(h_slabs, pos2d, starts, counts, padded, n_rows, tmc)
    ys = _moe_experts(xs, tile_expert, n_used, wg, wu, wd, tme)
    return _combine(pos2d, ys, wts, h, g, b, alpha, tmc)


def kernel(x, meta, w_in, b_gate, gla_w_a2, gla_b_a, gla_norm_g, ret_norm_g, ret_norm_b, w_pa, w_pb, w_o,
           ln1_g, ln1_b, ffn_w_gate, ffn_w_up, ffn_w_down, moe_w_router, moe_w_gate, moe_w_up, moe_w_down,
           ln2_g, ln2_b):
    bsz, seq, d = x.shape
    depth = w_in.shape[0]
    alpha = (2 * depth) ** 0.25
    lp = seq + N_META + PAD
    assert lp % CHUNK == 0 and d == VW
    m = bsz * lp

    h = jnp.concatenate([jnp.zeros((bsz, PAD, d), x.dtype),
                         jnp.broadcast_to(meta[None].astype(x.dtype), (bsz, N_META, d)), x], axis=1)
    h = h.reshape(m, d)
    tables = _retention_tables(lp)
    keep = jnp.asarray(np.tile((np.arange(lp) >= PAD).astype(np.float32), bsz).reshape(m, 1))

    c0 = 2 * QK + 2 * VW
    row = lambda a: a.reshape(1, -1)
    for l in range(depth):
        w = w_in[l]
        w_perm = jnp.concatenate(
            [w[:, :c0], w[:, c0 + GLA_RANK:], w[:, c0:c0 + GLA_RANK],
             jnp.zeros((d, V7X_MXU_COLS - GLA_RANK), w.dtype)], axis=1).astype(BF16)
        wa2 = jnp.concatenate([gla_w_a2[l], jnp.zeros((V7X_LANES - GLA_RANK, QK), F32)], axis=0).astype(BF16)

        proj = _in_proj(h, keep, w_perm)
        ya, yb = _mixer(proj.reshape(bsz, lp, N_PROJ), wa2, row(gla_b_a[l]), row(gla_norm_g[l]),
                        row(ret_norm_g[l]), row(ret_norm_b[l]), tables)
        is_moe = l % 2 == 1
        mixed = _mix_out(ya.reshape(m, VW), yb.reshape(m, VW), proj, h,
                         w_pa[l].astype(BF16), w_pb[l].astype(BF16), w_o[l].astype(BF16),
                         row(b_gate[l, 0]), row(b_gate[l, 1]), row(ln1_g[l]), row(ln1_b[l]), alpha, is_moe)
        h = mixed[0]
        if l % 2 == 0:
            j = l // 2
            h = _ffn(h, ffn_w_gate[j].astype(BF16), ffn_w_up[j].astype(BF16), ffn_w_down[j].astype(BF16),
                     row(ln2_g[l]), row(ln2_b[l]), alpha)
        else:
            j = l // 2
            h = _moe_layer(h, mixed[1], moe_w_router[j], moe_w_gate[j].astype(BF16), moe_w_up[j].astype(BF16),
                           moe_w_down[j].astype(BF16), row(ln2_g[l]), row(ln2_b[l]), alpha)
    return h.reshape(bsz, lp, d)[:, PAD + N_META:]
```

```python
import functools
import math

import jax
import jax.numpy as jnp
import numpy as np
from jax import lax
from jax.experimental import pallas as pl
from jax.experimental.pallas import tpu as pltpu

F32 = jnp.float32
BF16 = jnp.bfloat16

N_META = 16
CHUNK = 64
PAD = CHUNK - N_META
HEADS = 4
HEADS_PER_STEP = 2
CHUNKS_PER_TRIP = 3
DK = 128
DV = 256
GLA_RANK = 16
GLA_TAU = 16.0
ROPE_BASE = 10000.0
N_EXPERTS = 8
LN_EPS = 1e-5
GN_EPS = 1e-6

V7X_LANES = 128
V7X_BF16_SUBLANES = 16
V7X_MXU_COLS = 256
V7X_VMEM_LIMIT_BYTES = 56 * 1024 * 1024

SLAB = 8
ZERO_ROWS = 128

QK = HEADS * DK
VW = HEADS * DV
COL_GQ, COL_GK, COL_GV, COL_GR = 0, QK, 2 * QK, 2 * QK + VW
COL_RQ = 2 * QK + 2 * VW
COL_RK, COL_RV, COL_RG = COL_RQ + QK, COL_RQ + 2 * QK, COL_RQ + 2 * QK + VW
COL_GATE_A = COL_RQ + 2 * QK + 2 * VW
COL_GATE_B = COL_GATE_A + QK + QK
COL_GA = COL_GATE_B + QK + QK
N_PROJ = COL_GA + V7X_MXU_COLS


def _divisor_tile(n, cap, multiple):
    best = None
    for t in range(multiple, min(n, cap) + 1, multiple):
        if n % t == 0:
            best = t
    assert best is not None, (n, cap, multiple)
    return best


def _params(*sem):
    return pltpu.CompilerParams(dimension_semantics=sem, vmem_limit_bytes=V7X_VMEM_LIMIT_BYTES)


def _resident(shape):
    return pl.BlockSpec(shape, lambda *_: (0,) * len(shape), pipeline_mode=pl.Buffered(1))


def _row_blocks(rows):
    for n in (4, 3, 2):
        if rows % n == 0 and (rows // n) % V7X_BF16_SUBLANES == 0:
            return [slice(k * rows // n, (k + 1) * rows // n) for k in range(n)]
    return [slice(0, rows)]


def _layer_norm(r, g, b):
    mu = jnp.mean(r, axis=-1, keepdims=True)
    d = r - mu
    var = jnp.mean(d * d, axis=-1, keepdims=True)
    return d * lax.rsqrt(var + LN_EPS) * g + b


def _silu(x):
    return x * jax.nn.sigmoid(x)


def _in_proj_body(x_ref, keep_ref, w_ref, o_ref, xb_ref):
    @pl.when(pl.program_id(1) == 0)
    def _():
        xb_ref[...] = jnp.where(keep_ref[...] > 0.0, x_ref[...], 0.0).astype(BF16)

    xb = xb_ref[...]
    for c0 in range(0, o_ref.shape[1], V7X_MXU_COLS):
        cols = slice(c0, c0 + V7X_MXU_COLS)
        o_ref[:, cols] = jnp.dot(xb, w_ref[:, cols], preferred_element_type=F32).astype(BF16)


def _in_proj(h, keep, w):
    m, d = h.shape
    n = w.shape[1]
    tm = _divisor_tile(m, 1056, V7X_BF16_SUBLANES)
    tn = _divisor_tile(n, 2816, V7X_MXU_COLS)
    return pl.pallas_call(
        _in_proj_body,
        grid=(m // tm, n // tn),
        in_specs=[pl.BlockSpec((tm, d), lambda i, j: (i, 0)), pl.BlockSpec((tm, 1), lambda i, j: (i, 0)),
                  pl.BlockSpec((d, tn), lambda i, j: (0, j))],
        out_specs=pl.BlockSpec((tm, tn), lambda i, j: (i, j)),
        out_shape=jax.ShapeDtypeStruct((m, n), BF16),
        scratch_shapes=[pltpu.VMEM((tm, d), BF16)],
        compiler_params=_params("arbitrary", "arbitrary"),
        name="in_proj",
    )(h, keep, w)


def _mixer_body(gq_ref, gk_ref, gv_ref, gr_ref, ga_ref, rq_ref, rk_ref, rv_ref, rg_ref,
                wa2_ref, ba_ref, gng_ref, rng_ref, rnb_ref, cos_ref, sin_ref,
                dmat_ref, xi_ref, zeta_ref, rdec_ref,
                ya_ref, yb_ref, sa_ref, sb_ref, b_ref, oa_ref, ob_ref, *, n_groups, gsz, hps):
    c_ = CHUNK
    r_ = gsz * c_
    heads = range(hps)
    sa_ref[...] = jnp.zeros_like(sa_ref)
    sb_ref[...] = jnp.zeros_like(sb_ref)
    row = lax.broadcasted_iota(jnp.int32, (c_, c_), 0)
    col = lax.broadcasted_iota(jnp.int32, (c_, c_), 1)
    causal = row >= col
    grow = lax.broadcasted_iota(jnp.int32, (r_, r_), 0)
    gcol = lax.broadcasted_iota(jnp.int32, (r_, r_), 1)
    shift = int(math.log2(c_))
    gtri = jnp.where((jnp.right_shift(grow, shift) == jnp.right_shift(gcol, shift)) & (grow >= gcol),
                     1.0, 0.0).astype(BF16)
    wa2 = wa2_ref[...]
    ba = ba_ref[...]
    gng = gng_ref[...]
    rng = rng_ref[...]
    rnb = rnb_ref[...]
    nt = (((1,), (1,)), ((), ()))
    tn = (((0,), (0,)), ((), ()))
    mm = functools.partial(jnp.dot, preferred_element_type=F32)
    mmg = functools.partial(lax.dot_general, preferred_element_type=F32)
    chunks = [slice(g * c_, (g + 1) * c_) for g in range(gsz)]
    hk = lambda x, hh: x[:, hh * DK:(hh + 1) * DK]
    hv = lambda x, hh: x[:, hh * DV:(hh + 1) * DV]
    per_head = lambda xs: jnp.concatenate(xs, axis=1)

    def group_rows(j):
        return pl.ds(pl.multiple_of(j * r_, c_), r_)

    def gate_logits(j):
        return mm(ga_ref[0, group_rows(j), :], wa2) + ba

    def log_decay(z, j):
        valid = j * r_ + lax.broadcasted_iota(jnp.int32, (r_, 1), 0) >= PAD
        log_sig = jnp.minimum(z, 0.0) - jnp.log(1.0 + jnp.exp(-jnp.abs(z)))
        la = jnp.where(valid, log_sig * (1.0 / GLA_TAU), 0.0)
        la_hi = la.astype(BF16)
        la_lo = (la - la_hi.astype(F32)).astype(BF16)
        return mm(gtri, la_hi) + mm(gtri, la_lo)

    def finish(j, slot):
        rows = group_rows(j)
        o = oa_ref[slot]
        on = []
        for hh in heads:
            o_h = hv(o, hh)
            ms = jnp.mean(o_h * o_h, axis=-1, keepdims=True)
            on.append(o_h * lax.rsqrt(ms + LN_EPS) * gng)
        ya_ref[0, rows, :] = _silu(gr_ref[0, rows, :]) * per_head(on).astype(BF16)
        ob = ob_ref[slot]
        obn = []
        for hh in heads:
            ob_h = hv(ob, hh)
            dd = ob_h - jnp.mean(ob_h, axis=-1, keepdims=True)
            var = jnp.mean(dd * dd, axis=-1, keepdims=True)
            obn.append(dd * lax.rsqrt(var + GN_EPS))
        yb_ref[0, rows, :] = _silu(rg_ref[0, rows, :]) * (per_head(obn) * rng + rnb).astype(BF16)

    oa_ref[...] = jnp.zeros_like(oa_ref)
    ob_ref[...] = jnp.zeros_like(ob_ref)
    b_ref[0] = log_decay(gate_logits(0), 0)

    def group(gi, carry):
        slot = lax.rem(gi, 2)
        nxt = jnp.minimum(gi + 1, n_groups - 1)
        z_next = gate_logits(nxt)

        rows = group_rows(gi)
        cos = cos_ref[rows, :]
        sin = sin_ref[rows, :]
        rq = rq_ref[0, rows, :].astype(F32)
        rk = rk_ref[0, rows, :].astype(F32)
        vb = rv_ref[0, rows, :]
        qr, s_b, st_b = [], [], []
        for hh in heads:
            rq_h, rk_h = hk(rq, hh), hk(rk, hh)
            qr.append(rq_h * cos + pltpu.roll(rq_h, DK // 2, 1) * sin)
            kr = (rk_h * cos + pltpu.roll(rk_h, DK // 2, 1) * sin) * (DK ** -0.5)
            qr_b, kr_b = qr[hh].astype(BF16), kr.astype(BF16)
            s_b.append([(mmg(qr_b[cs], kr_b[cs], nt) * dmat_ref[hh]).astype(BF16) for cs in chunks])
            st, prev = sb_ref[hh], []
            for cs in chunks:
                prev.append(st.astype(BF16))
                st = st * rdec_ref[hh] + mmg((kr[cs] * zeta_ref[hh]).astype(BF16), hv(vb, hh)[cs], tn)
            sb_ref[hh] = st
            st_b.append(prev)

        b = b_ref[slot]
        q = gq_ref[0, rows, :].astype(F32) * (DK ** -0.5)
        k = gk_ref[0, rows, :].astype(F32)
        v = gv_ref[0, rows, :]
        q_in = (q * jnp.exp(b)).astype(BF16)
        k_in = (k * jnp.exp(-b)).astype(BF16)
        s_a, st_a = [], []
        for hh in heads:
            s_a.append([jnp.where(causal, mmg(hk(q_in, hh)[cs], hk(k_in, hh)[cs], nt), 0.0).astype(BF16)
                        for cs in chunks])
            st, prev = sa_ref[hh], []
            for cs in chunks:
                b_c = hk(b, hh)[cs]
                b_last = b_c[c_ - 1:c_, :]
                k_tail = (hk(k, hh)[cs] * jnp.exp(b_last - b_c)).astype(BF16)
                dcol = jnp.transpose(jnp.broadcast_to(jnp.exp(b_last), (DK, DK)))
                prev.append(st.astype(BF16))
                st = st * jnp.concatenate([dcol, dcol], axis=1) + mmg(k_tail, hv(v, hh)[cs], tn)
            sa_ref[hh] = st
            st_a.append(prev)

        b_ref[1 - slot] = log_decay(z_next, nxt)
        finish(jnp.maximum(gi - 1, 0), 1 - slot)

        o_a, o_b = [], []
        for hh in heads:
            o_b.append(jnp.concatenate(
                [mm(s_b[hh][g], hv(vb, hh)[cs]) + mm((qr[hh][cs] * xi_ref[hh]).astype(BF16), st_b[hh][g])
                 for g, cs in enumerate(chunks)], axis=0))
        for hh in heads:
            o_a.append(jnp.concatenate(
                [mm(s_a[hh][g], hv(v, hh)[cs]) + mm(hk(q_in, hh)[cs], st_a[hh][g])
                 for g, cs in enumerate(chunks)], axis=0))
        oa_ref[slot] = per_head(o_a)
        ob_ref[slot] = per_head(o_b)
        return carry

    lax.fori_loop(0, n_groups, group, 0)
    finish(n_groups - 1, (n_groups - 1) % 2)


def _retention_tables(lp):
    half = DK // 2
    pos = np.arange(lp, dtype=np.float64) - PAD
    inv = ROPE_BASE ** (-np.arange(half, dtype=np.float64) / half)
    ang = pos[:, None] * inv[None, :]
    cos = np.concatenate([np.cos(ang), np.cos(ang)], axis=1)
    sin = np.concatenate([-np.sin(ang), np.sin(ang)], axis=1)
    lg = np.log(1.0 - 2.0 ** (-5.0 - np.arange(HEADS, dtype=np.float64)))
    idx = np.arange(CHUNK, dtype=np.float64)
    rel = idx[:, None] - idx[None, :]
    dmat = np.where(rel >= 0, np.exp(lg[:, None, None] * np.maximum(rel, 0.0)), 0.0)
    xi = np.broadcast_to(np.exp(lg[:, None] * (idx + 1.0))[:, :, None], (HEADS, CHUNK, DK))
    zeta = np.broadcast_to(np.exp(lg[:, None] * (CHUNK - 1.0 - idx))[:, :, None], (HEADS, CHUNK, DK))
    rdec = np.broadcast_to(np.exp(lg * CHUNK)[:, None, None], (HEADS, 1, DV))
    f = lambda a: jnp.asarray(np.ascontiguousarray(a), F32)
    return f(cos), f(sin), f(dmat), f(xi), f(zeta), f(rdec)


def _mixer(proj3, wa2, ba, gng, rng, rnb, tables):
    bsz, lp, _ = proj3.shape
    cos, sin, dmat, xi, zeta, rdec = tables
    n_chunks = lp // CHUNK
    gsz = CHUNKS_PER_TRIP if n_chunks % CHUNKS_PER_TRIP == 0 else 1
    hps = HEADS_PER_STEP
    kw, vw = hps * DK, hps * DV
    qk = lambda col: pl.BlockSpec((1, lp, kw), lambda b, h, c=col // kw: (b, 0, c + h))
    vv = lambda col: pl.BlockSpec((1, lp, vw), lambda b, h, c=col // vw: (b, 0, c + h))
    in_specs = [
        qk(COL_GQ), qk(COL_GK), vv(COL_GV), vv(COL_GR),
        pl.BlockSpec((1, lp, V7X_LANES), lambda b, h: (b, 0, COL_GA // V7X_LANES)),
        qk(COL_RQ), qk(COL_RK), vv(COL_RV), vv(COL_RG),
        pl.BlockSpec((V7X_LANES, kw), lambda b, h: (0, h)),
        pl.BlockSpec((1, kw), lambda b, h: (0, h)),
        pl.BlockSpec((1, DV), lambda b, h: (0, 0)),
        pl.BlockSpec((1, vw), lambda b, h: (0, h)),
        pl.BlockSpec((1, vw), lambda b, h: (0, h)),
        pl.BlockSpec((lp, DK), lambda b, h: (0, 0)),
        pl.BlockSpec((lp, DK), lambda b, h: (0, 0)),
        pl.BlockSpec((hps, CHUNK, CHUNK), lambda b, h: (h, 0, 0)),
        pl.BlockSpec((hps, CHUNK, DK), lambda b, h: (h, 0, 0)),
        pl.BlockSpec((hps, CHUNK, DK), lambda b, h: (h, 0, 0)),
        pl.BlockSpec((hps, 1, DV), lambda b, h: (h, 0, 0)),
    ]
    out_spec = pl.BlockSpec((1, lp, vw), lambda b, h: (b, 0, h))
    return pl.pallas_call(
        functools.partial(_mixer_body, n_groups=n_chunks // gsz, gsz=gsz, hps=hps),
        grid=(bsz, HEADS // hps),
        in_specs=in_specs,
        out_specs=[out_spec, out_spec],
        out_shape=[jax.ShapeDtypeStruct((bsz, lp, VW), BF16)] * 2,
        scratch_shapes=[pltpu.VMEM((hps, DK, DV), F32), pltpu.VMEM((hps, DK, DV), F32),
                        pltpu.VMEM((2, gsz * CHUNK, kw), F32),
                        pltpu.VMEM((2, gsz * CHUNK, vw), F32), pltpu.VMEM((2, gsz * CHUNK, vw), F32)],
        compiler_params=_params("arbitrary", "arbitrary"),
        name="mixer",
    )(*([proj3] * 4), proj3, *([proj3] * 4), wa2, ba, gng, rng, rnb, cos, sin, dmat, xi, zeta, rdec)


def _mix_out_body(ya_ref, yb_ref, gta_ref, gtb_ref, h_ref, wpa_ref, wpb_ref, wo_ref,
                  bga_ref, bgb_ref, g_ref, b_ref, o_ref, *rest, alpha, with_slabs):
    mg_ref = rest[-1]
    for c0 in range(0, mg_ref.shape[1], V7X_MXU_COLS):
        cols = slice(c0, c0 + V7X_MXU_COLS)
        y_a = jnp.dot(ya_ref[...], wpa_ref[:, cols], preferred_element_type=F32)
        y_b = jnp.dot(yb_ref[...], wpb_ref[:, cols], preferred_element_type=F32)
        mg_ref[:, cols] = (jax.nn.sigmoid(gta_ref[:, cols].astype(F32) + bga_ref[:, cols]) * y_a
                           + jax.nn.sigmoid(gtb_ref[:, cols].astype(F32) + bgb_ref[:, cols]) * y_b).astype(BF16)
    for rs in _row_blocks(mg_ref.shape[0]):
        mix = jnp.dot(mg_ref[rs, :], wo_ref[...], preferred_element_type=F32)
        out = _layer_norm(alpha * h_ref[rs, :] + mix, g_ref[...], b_ref[...])
        o_ref[rs, :] = out
        if with_slabs:
            _matrix_to_slabs(out, rest[0], rs.start)


def _mix_out(ya, yb, proj, h, wpa, wpb, wo, bga, bgb, g, b, alpha, with_slabs):
    m, d = h.shape
    tm = _divisor_tile(m, 1056, V7X_BF16_SUBLANES)
    rows = lambda: pl.BlockSpec((tm, d), lambda i: (i, 0))
    out_specs = [rows()]
    out_shape = [jax.ShapeDtypeStruct((m, d), F32)]
    if with_slabs:
        assert d == SLAB * V7X_LANES
        out_specs.append(pl.BlockSpec((tm * SLAB, V7X_LANES), lambda i: (i, 0)))
        out_shape.append(jax.ShapeDtypeStruct((m * SLAB, V7X_LANES), F32))
    return pl.pallas_call(
        functools.partial(_mix_out_body, alpha=alpha, with_slabs=with_slabs),
        grid=(m // tm,),
        in_specs=[rows(), rows(),
                  pl.BlockSpec((tm, d), lambda i: (i, COL_GATE_A // d)),
                  pl.BlockSpec((tm, d), lambda i: (i, COL_GATE_B // d)),
                  rows(), _resident((d, d)), _resident((d, d)), _resident((d, d)),
                  _resident((1, d)), _resident((1, d)), _resident((1, d)), _resident((1, d))],
        out_specs=out_specs,
        out_shape=out_shape,
        scratch_shapes=[pltpu.VMEM((tm, d), BF16)],
        compiler_params=_params("arbitrary"),
        name="mix_out",
    )(ya, yb, proj, proj, h, wpa, wpb, wo, bga, bgb, g, b)


def _swiglu_hidden(xb, wg_ref, wu_ref, hid_ref, widx=()):
    dff = hid_ref.shape[1]
    for c0 in range(0, dff, V7X_MXU_COLS):
        cols = slice(c0, c0 + V7X_MXU_COLS)
        gate = jnp.dot(xb, wg_ref[widx + (slice(None), cols)], preferred_element_type=F32)
        up = jnp.dot(xb, wu_ref[widx + (slice(None), cols)], preferred_element_type=F32)
        hid_ref[:, cols] = (_silu(gate) * up).astype(BF16)


def _ffn_body(x_ref, wg_ref, wu_ref, wd_ref, g_ref, b_ref, o_ref, hid_ref, *, alpha):
    _swiglu_hidden(x_ref[...].astype(BF16), wg_ref, wu_ref, hid_ref)
    for rs in _row_blocks(hid_ref.shape[0]):
        f = jnp.dot(hid_ref[rs, :], wd_ref[...], preferred_element_type=F32)
        o_ref[rs, :] = _layer_norm(alpha * x_ref[rs, :] + f, g_ref[...], b_ref[...])


def _ffn(h, wg, wu, wd, g, b, alpha):
    m, d = h.shape
    dff = wg.shape[1]
    assert dff % V7X_MXU_COLS == 0
    tm = _divisor_tile(m, 1056, V7X_BF16_SUBLANES)
    return pl.pallas_call(
        functools.partial(_ffn_body, alpha=alpha),
        grid=(m // tm,),
        in_specs=[pl.BlockSpec((tm, d), lambda i: (i, 0)),
                  _resident((d, dff)), _resident((d, dff)), _resident((dff, d)),
                  _resident((1, d)), _resident((1, d))],
        out_specs=pl.BlockSpec((tm, d), lambda i: (i, 0)),
        out_shape=jax.ShapeDtypeStruct((m, d), F32),
        scratch_shapes=[pltpu.VMEM((tm, dff), BF16)],
        compiler_params=_params("arbitrary"),
        name="ffn",
    )(h, wg, wu, wd, g, b)


def _router_body(h_ref, wr_ref, idx_ref, wt_ref, rank_ref, cnt_ref, tri_ref, run_ref):
    ne = N_EXPERTS
    tm = h_ref.shape[0]

    @pl.when(pl.program_id(0) == 0)
    def _():
        r = lax.broadcasted_iota(jnp.int32, (tm, tm), 0)
        c = lax.broadcasted_iota(jnp.int32, (tm, tm), 1)
        tri_ref[...] = jnp.where(r > c, 1.0, 0.0).astype(BF16)
        run_ref[...] = jnp.zeros_like(run_ref)

    h = h_ref[...]
    h_hi = h.astype(BF16)
    h_lo = (h - h_hi.astype(F32)).astype(BF16)
    w2 = wr_ref[...]
    both = jnp.dot(h_hi, w2, preferred_element_type=F32)
    logits = both[:, :ne] + both[:, ne:] + jnp.dot(h_lo, w2[:, :ne], preferred_element_type=F32)
    e = lax.broadcasted_iota(jnp.int32, logits.shape, 1)
    m1 = jnp.max(logits, axis=1, keepdims=True)
    i1 = jnp.min(jnp.where(logits == m1, e, ne), axis=1, keepdims=True)
    rest = jnp.where(e == i1, -jnp.inf, logits)
    m2 = jnp.max(rest, axis=1, keepdims=True)
    i2 = jnp.min(jnp.where(rest == m2, e, ne), axis=1, keepdims=True)
    e2 = jnp.exp(m2 - m1)
    w1 = 1.0 / (1.0 + e2)
    idx_ref[...] = jnp.concatenate([i1, i2], axis=1)
    wt_ref[...] = jnp.concatenate([w1, e2 * w1], axis=1)

    one1 = jnp.where(e == i1, 1.0, 0.0)
    one2 = jnp.where(e == i2, 1.0, 0.0)
    both_slots = one1 + one2
    before = jnp.dot(tri_ref[...], both_slots.astype(BF16), preferred_element_type=F32) + run_ref[...]
    r1 = jnp.sum(one1 * before, axis=1, keepdims=True)
    r2 = jnp.sum(one2 * before, axis=1, keepdims=True)
    rank_ref[...] = jnp.concatenate([r1, r2], axis=1).astype(jnp.int32)
    run_ref[...] += jnp.sum(both_slots, axis=0, keepdims=True)
    cnt_ref[...] = run_ref[...].astype(jnp.int32)


def _router(h, wr):
    m, d = h.shape
    tm = _divisor_tile(m, 1024, V7X_BF16_SUBLANES)
    w_hi = wr.astype(BF16)
    w2 = jnp.concatenate([w_hi, (wr - w_hi.astype(F32)).astype(BF16)], axis=1)
    per_token = lambda: pl.BlockSpec((tm, 2), lambda i: (i, 0))
    return pl.pallas_call(
        _router_body,
        grid=(m // tm,),
        in_specs=[pl.BlockSpec((tm, d), lambda i: (i, 0)), _resident((d, 2 * N_EXPERTS))],
        out_specs=[per_token(), per_token(), per_token(), pl.BlockSpec((1, N_EXPERTS), lambda i: (0, 0))],
        out_shape=[jax.ShapeDtypeStruct((m, 2), jnp.int32), jax.ShapeDtypeStruct((m, 2), F32),
                   jax.ShapeDtypeStruct((m, 2), jnp.int32), jax.ShapeDtypeStruct((1, N_EXPERTS), jnp.int32)],
        scratch_shapes=[pltpu.VMEM((tm, tm), BF16), pltpu.VMEM((1, N_EXPERTS), F32)],
        compiler_params=_params("arbitrary"),
        name="router",
    )(h, w2)


def _index_prefetch(idx_hbm, idx_smem, isem, i, n):
    width = idx_hbm.shape[1]
    slot = lax.rem(i, 2)
    half = lambda sl: idx_smem.at[pl.ds(pl.multiple_of(sl * width, width), width)]

    @pl.when(i == 0)
    def _():
        pltpu.make_async_copy(idx_hbm.at[0], half(0), isem.at[0]).start()

    pltpu.make_async_copy(idx_hbm.at[i], half(slot), isem.at[slot]).wait()

    @pl.when(i + 1 < n)
    def _():
        pltpu.make_async_copy(idx_hbm.at[i + 1], half(1 - slot), isem.at[1 - slot]).start()

    return slot * width


def _slab_at(off):
    return pl.ds(pl.multiple_of(off, SLAB), SLAB)


def _slab_rows(p):
    return pl.ds(pl.multiple_of(p * SLAB, SLAB), SLAB)


def _slabs_to_matrix(ref2, rows):
    return jnp.concatenate([ref2[pl.ds(g, rows, stride=SLAB), :] for g in range(SLAB)], axis=1)


def _matrix_to_slabs(x, ref2, row0=0):
    for g in range(SLAB):
        ref2[pl.ds(row0 * SLAB + g, x.shape[0], stride=SLAB), :] = x[:, g * V7X_LANES:(g + 1) * V7X_LANES]


def _dispatch_body(st_ref, cnt_ref, pd_ref, pos_hbm, h3_ref, xs_hbm, idx_smem, zbuf, isem, sem, psem,
                   *, tm, n_rows):
    i = pl.program_id(0)
    n = pl.num_programs(0)
    base = _index_prefetch(pos_hbm, idx_smem, isem, i, n)

    def issue(r, carry):
        src = h3_ref.at[_slab_rows(r)]
        pltpu.make_async_copy(src, xs_hbm.at[_slab_at(idx_smem[base + r])], sem).start()
        pltpu.make_async_copy(src, xs_hbm.at[_slab_at(idx_smem[base + tm + r])], sem).start(priority=1)
        return carry

    lax.fori_loop(0, tm, issue, 0, unroll=8)
    pltpu.make_async_copy(h3_ref, xs_hbm.at[pl.ds(0, tm * SLAB)], sem).wait()
    pltpu.make_async_copy(h3_ref, xs_hbm.at[pl.ds(0, tm * SLAB)], sem).wait()

    @pl.when(i == n - 1)
    def _():
        zbuf[...] = jnp.zeros_like(zbuf)
        zrow = zbuf.at[pl.ds(0, SLAB)]
        for e in range(N_EXPERTS):
            lo = st_ref[e] + cnt_ref[e]
            npad = pd_ref[e] - cnt_ref[e]

            def fill(r, carry):
                pltpu.make_async_copy(zrow, xs_hbm.at[_slab_rows(lo + r)], psem).start()
                return carry

            def drain(r, carry):
                pltpu.make_async_copy(zrow, xs_hbm.at[_slab_rows(lo + r)], psem).wait()
                return carry

            lax.fori_loop(0, npad, fill, 0)
            lax.fori_loop(0, npad, drain, 0)

        zr = zbuf.shape[0] // SLAB
        tail = st_ref[N_EXPERTS - 1] + pd_ref[N_EXPERTS - 1]
        nblk = (n_rows - tail) // zr

        def block(k):
            return xs_hbm.at[pl.ds(pl.multiple_of((tail + k * zr) * SLAB, SLAB), zr * SLAB)]

        def fill_block(k, carry):
            pltpu.make_async_copy(zbuf, block(k), psem).start()
            return carry

        def drain_block(k, carry):
            pltpu.make_async_copy(zbuf, block(k), psem).wait()
            return carry

        lax.fori_loop(0, nblk, fill_block, 0)
        lax.fori_loop(0, nblk, drain_block, 0)


def _dispatch(h3, pos2d, starts, counts, padded, n_rows, tm):
    m = h3.shape[0] // SLAB
    grid_spec = pltpu.PrefetchScalarGridSpec(
        num_scalar_prefetch=3,
        grid=(m // tm,),
        in_specs=[pl.BlockSpec(memory_space=pl.ANY),
                  pl.BlockSpec((tm * SLAB, V7X_LANES), lambda i, *_: (i, 0))],
        out_specs=pl.BlockSpec(memory_space=pl.ANY),
        scratch_shapes=[pltpu.SMEM((4 * tm,), jnp.int32), pltpu.VMEM((ZERO_ROWS * SLAB, V7X_LANES), F32),
                        pltpu.SemaphoreType.DMA((2,)), pltpu.SemaphoreType.DMA, pltpu.SemaphoreType.DMA],
    )
    return pl.pallas_call(
        functools.partial(_dispatch_body, tm=tm, n_rows=n_rows),
        grid_spec=grid_spec,
        out_shape=jax.ShapeDtypeStruct((n_rows * SLAB, V7X_LANES), F32),
        compiler_params=_params("arbitrary"),
        name="moe_dispatch",
    )(starts, counts, padded, pos2d, h3)


def _moe_body(te_ref, nu_ref, x_ref, wg_ref, wu_ref, wd_ref, o_ref, hid_ref):
    @pl.when(pl.program_id(0) < nu_ref[0])
    def _():
        xb = _slabs_to_matrix(x_ref, hid_ref.shape[0]).astype(BF16)
        _swiglu_hidden(xb, wg_ref, wu_ref, hid_ref, widx=(0,))
        for rs in _row_blocks(hid_ref.shape[0]):
            _matrix_to_slabs(jnp.dot(hid_ref[rs, :], wd_ref[0], preferred_element_type=F32), o_ref, rs.start)

    @pl.when(pl.program_id(0) >= nu_ref[0])
    def _():
        o_ref[...] = jnp.zeros_like(o_ref)


def _moe_experts(xs, tile_expert, n_used, wg, wu, wd, tme):
    r = xs.shape[0] // SLAB
    d, dff = wg.shape[1], wg.shape[2]
    assert dff % V7X_MXU_COLS == 0
    expert = lambda shape: pl.BlockSpec((1,) + shape, lambda i, te, nu: (te[i], 0, 0), pipeline_mode=pl.Buffered(1))
    grid_spec = pltpu.PrefetchScalarGridSpec(
        num_scalar_prefetch=2,
        grid=(r // tme,),
        in_specs=[pl.BlockSpec((tme * SLAB, V7X_LANES), lambda i, te, nu: (i, 0)),
                  expert((d, dff)), expert((d, dff)), expert((dff, d))],
        out_specs=pl.BlockSpec((tme * SLAB, V7X_LANES), lambda i, te, nu: (i, 0)),
        scratch_shapes=[pltpu.VMEM((tme, dff), BF16)],
    )
    return pl.pallas_call(
        _moe_body,
        grid_spec=grid_spec,
        out_shape=jax.ShapeDtypeStruct(xs.shape, F32),
        compiler_params=_params("arbitrary"),
        name="moe_experts",
    )(tile_expert, n_used, xs, wg, wu, wd)


def _combine_body(pos_hbm, ys_hbm, wt_ref, h_ref, g_ref, b_ref, o_ref, ybuf, idx_smem, isem, sem, *, tm, alpha):
    i = pl.program_id(0)
    base = _index_prefetch(pos_hbm, idx_smem, isem, i, pl.num_programs(0))

    def issue(r, carry):
        pltpu.make_async_copy(ys_hbm.at[_slab_at(idx_smem[base + r])], ybuf.at[0, _slab_rows(r)], sem).start()
        pltpu.make_async_copy(ys_hbm.at[_slab_at(idx_smem[base + tm + r])], ybuf.at[1, _slab_rows(r)],
                              sem).start(priority=1)
        return carry

    lax.fori_loop(0, tm, issue, 0, unroll=8)
    pltpu.make_async_copy(ys_hbm.at[pl.ds(0, tm * SLAB)], ybuf.at[0], sem).wait()
    pltpu.make_async_copy(ys_hbm.at[pl.ds(0, tm * SLAB)], ybuf.at[1], sem).wait()
    for rs in _row_blocks(tm):
        n = rs.stop - rs.start
        rows = lambda k: ybuf.at[k, pl.ds(rs.start * SLAB, n * SLAB)]
        wt = wt_ref[rs, :]
        out = wt[:, 0:1] * _slabs_to_matrix(rows(0), n) + wt[:, 1:2] * _slabs_to_matrix(rows(1), n)
        o_ref[rs, :] = _layer_norm(alpha * h_ref[rs, :] + out, g_ref[...], b_ref[...])


def _combine(pos2d, ys, wt, h, g, b, alpha, tm):
    m, d = h.shape
    return pl.pallas_call(
        functools.partial(_combine_body, tm=tm, alpha=alpha),
        grid=(m // tm,),
        in_specs=[pl.BlockSpec(memory_space=pl.ANY), pl.BlockSpec(memory_space=pl.ANY),
                  pl.BlockSpec((tm, 2), lambda i: (i, 0)),
                  pl.BlockSpec((tm, d), lambda i: (i, 0)),
                  pl.BlockSpec((1, d), lambda i: (0, 0)), pl.BlockSpec((1, d), lambda i: (0, 0))],
        out_specs=pl.BlockSpec((tm, d), lambda i: (i, 0)),
        out_shape=jax.ShapeDtypeStruct((m, d), F32),
        scratch_shapes=[pltpu.VMEM((2, tm * SLAB, V7X_LANES), F32), pltpu.SMEM((4 * tm,), jnp.int32),
                        pltpu.SemaphoreType.DMA((2,)), pltpu.SemaphoreType.DMA],
        compiler_params=_params("arbitrary"),
        name="moe_combine",
    )(pos2d, ys, wt, h, g, b)


def _moe_layer(h, h_slabs, wr, wg, wu, wd, g, b, alpha):
    m, d = h.shape
    assert d == SLAB * V7X_LANES
    tme = _divisor_tile(m, 1024, ZERO_ROWS)
    tmc = _divisor_tile(m, 1024, 8)
    idx, wts, rank, counts = _router(h, wr)

    counts = counts[0]
    padded = ((counts + tme - 1) // tme) * tme
    ends = jnp.cumsum(padded)
    starts = ends - padded
    pos = jnp.sum(jnp.where(idx[:, :, None] == jnp.arange(N_EXPERTS, dtype=jnp.int32), starts, 0), axis=-1) + rank
    n_rows = 2 * m + N_EXPERTS * tme
    n_tiles = n_rows // tme
    tile_start = jnp.arange(n_tiles, dtype=jnp.int32) * tme
    tile_expert = jnp.minimum(jnp.sum(tile_start[:, None] >= ends[None, :], axis=1), N_EXPERTS - 1).astype(jnp.int32)
    n_used = (ends[-1] // tme).astype(jnp.int32).reshape(1)
    pos2d = (pos * SLAB).reshape(m // tmc, tmc, 2).transpose(0, 2, 1).reshape(m // tmc, 2 * tmc)

    xs = _dispatch(h_slabs, pos2d, starts, counts, padded, n_rows, tmc)
    ys = _moe_experts(xs, tile_expert, n_used, wg, wu, wd, tme)
    return _combine(pos2d, ys, wts, h, g, b, alpha, tmc)


def kernel(x, meta, w_in, b_gate, gla_w_a2, gla_b_a, gla_norm_g, ret_norm_g, ret_norm_b, w_pa, w_pb, w_o,
           ln1_g, ln1_b, ffn_w_gate, ffn_w_up, ffn_w_down, moe_w_router, moe_w_gate, moe_w_up, moe_w_down,
           ln2_g, ln2_b):
    bsz, seq, d = x.shape
    depth = w_in.shape[0]
    alpha = (2 * depth) ** 0.25
    lp = seq + N_META + PAD
    assert lp % CHUNK == 0 and d == VW
    m = bsz * lp

    h = jnp.concatenate([jnp.zeros((bsz, PAD, d), x.dtype),
                         jnp.broadcast_to(meta[None].astype(x.dtype), (bsz, N_META, d)), x], axis=1)
    h = h.reshape(m, d)
    tables = _retention_tables(lp)
    keep = jnp.asarray(np.tile((np.arange(lp) >= PAD).astype(np.float32), bsz).reshape(m, 1))

    c0 = 2 * QK + 2 * VW
    row = lambda a: a.reshape(1, -1)
    for l in range(depth):
        w = w_in[l]
        w_perm = jnp.concatenate(
            [w[:, :c0], w[:, c0 + GLA_RANK:], w[:, c0:c0 + GLA_RANK],
             jnp.zeros((d, V7X_MXU_COLS - GLA_RANK), w.dtype)], axis=1).astype(BF16)
        wa2 = jnp.concatenate([gla_w_a2[l], jnp.zeros((V7X_LANES - GLA_RANK, QK), F32)], axis=0).astype(BF16)

        proj = _in_proj(h, keep, w_perm)
        ya, yb = _mixer(proj.reshape(bsz, lp, N_PROJ), wa2, row(gla_b_a[l]), row(gla_norm_g[l]),
                        row(ret_norm_g[l]), row(ret_norm_b[l]), tables)
        is_moe = l % 2 == 1
        mixed = _mix_out(ya.reshape(m, VW), yb.reshape(m, VW), proj, h,
                         w_pa[l].astype(BF16), w_pb[l].astype(BF16), w_o[l].astype(BF16),
                         row(b_gate[l, 0]), row(b_gate[l, 1]), row(ln1_g[l]), row(ln1_b[l]), alpha, is_moe)
        h = mixed[0]
        if l % 2 == 0:
            j = l // 2
            h = _ffn(h, ffn_w_gate[j].astype(BF16), ffn_w_up[j].astype(BF16), ffn_w_down[j].astype(BF16),
                     row(ln2_g[l]), row(ln2_b[l]), alpha)
        else:
            j = l // 2
            h = _moe_layer(h, mixed[1], moe_w_router[j], moe_w_gate[j].astype(BF16), moe_w_up[j].astype(BF16),
                           moe_w_down[j].astype(BF16), row(ln2_g[l]), row(ln2_b[l]), alpha)
    return h.reshape(bsz, lp, d)[:, PAD + N_META:]
```

```python
import functools
import math

import jax
import jax.numpy as jnp
import numpy as np
from jax import lax
from jax.experimental import pallas as pl
from jax.experimental.pallas import tpu as pltpu

F32 = jnp.float32
BF16 = jnp.bfloat16

N_META = 16
CHUNK = 64
PAD = CHUNK - N_META
HEADS = 4
HEADS_PER_STEP = 2
CHUNKS_PER_TRIP = 3
DK = 128
DV = 256
GLA_RANK = 16
GLA_TAU = 16.0
ROPE_BASE = 10000.0
N_EXPERTS = 8
LN_EPS = 1e-5
GN_EPS = 1e-6

V7X_LANES = 128
V7X_BF16_SUBLANES = 16
V7X_MXU_COLS = 256
V7X_VMEM_LIMIT_BYTES = 56 * 1024 * 1024

SLAB = 8
ZERO_ROWS = 128

QK = HEADS * DK
VW = HEADS * DV
COL_GQ, COL_GK, COL_GV, COL_GR = 0, QK, 2 * QK, 2 * QK + VW
COL_RQ = 2 * QK + 2 * VW
COL_RK, COL_RV, COL_RG = COL_RQ + QK, COL_RQ + 2 * QK, COL_RQ + 2 * QK + VW
COL_GATE_A = COL_RQ + 2 * QK + 2 * VW
COL_GATE_B = COL_GATE_A + QK + QK
COL_GA = COL_GATE_B + QK + QK
N_PROJ = COL_GA + V7X_MXU_COLS


def _divisor_tile(n, cap, multiple):
    best = None
    for t in range(multiple, min(n, cap) + 1, multiple):
        if n % t == 0:
            best = t
    assert best is not None, (n, cap, multiple)
    return best


def _params(*sem):
    return pltpu.CompilerParams(dimension_semantics=sem, vmem_limit_bytes=V7X_VMEM_LIMIT_BYTES)


def _resident(shape):
    return pl.BlockSpec(shape, lambda *_: (0,) * len(shape), pipeline_mode=pl.Buffered(1))


def _resident_layer(shape, layer):
    return pl.BlockSpec((1,) + shape, lambda *_: (layer,) + (0,) * len(shape), pipeline_mode=pl.Buffered(1))


def _row_blocks(rows):
    for n in (4, 3, 2):
        if rows % n == 0 and (rows // n) % V7X_BF16_SUBLANES == 0:
            return [slice(k * rows // n, (k + 1) * rows // n) for k in range(n)]
    return [slice(0, rows)]


def _layer_norm(r, g, b):
    mu = jnp.mean(r, axis=-1, keepdims=True)
    d = r - mu
    var = jnp.mean(d * d, axis=-1, keepdims=True)
    return d * lax.rsqrt(var + LN_EPS) * g + b


def _silu(x):
    return x * jax.nn.sigmoid(x)


def _in_proj_body(x_ref, keep_ref, w_ref, o_ref, xb_ref):
    @pl.when(pl.program_id(1) == 0)
    def _():
        xb_ref[...] = jnp.where(keep_ref[...] > 0.0, x_ref[...], 0.0).astype(BF16)

    xb = xb_ref[...]
    for c0 in range(0, o_ref.shape[1], V7X_MXU_COLS):
        cols = slice(c0, c0 + V7X_MXU_COLS)
        o_ref[:, cols] = jnp.dot(xb, w_ref[:, cols], preferred_element_type=F32).astype(BF16)


def _in_proj(h, keep, w):
    m, d = h.shape
    n = w.shape[1]
    tm = _divisor_tile(m, 1056, V7X_BF16_SUBLANES)
    tn = _divisor_tile(n, 2816, V7X_MXU_COLS)
    return pl.pallas_call(
        _in_proj_body,
        grid=(m // tm, n // tn),
        in_specs=[pl.BlockSpec((tm, d), lambda i, j: (i, 0)), pl.BlockSpec((tm, 1), lambda i, j: (i, 0)),
                  pl.BlockSpec((d, tn), lambda i, j: (0, j))],
        out_specs=pl.BlockSpec((tm, tn), lambda i, j: (i, j)),
        out_shape=jax.ShapeDtypeStruct((m, n), BF16),
        scratch_shapes=[pltpu.VMEM((tm, d), BF16)],
        compiler_params=_params("arbitrary", "arbitrary"),
        name="in_proj",
    )(h, keep, w)


def _mixer_body(gq_ref, gk_ref, gv_ref, gr_ref, ga_ref, rq_ref, rk_ref, rv_ref, rg_ref,
                wa2_ref, ba_ref, gng_ref, rng_ref, rnb_ref, cos_ref, sin_ref,
                dmat_ref, xi_ref, zeta_ref, rdec_ref,
                ya_ref, yb_ref, sa_ref, sb_ref, b_ref, oa_ref, ob_ref, *, n_groups, gsz, hps):
    c_ = CHUNK
    r_ = gsz * c_
    heads = range(hps)
    sa_ref[...] = jnp.zeros_like(sa_ref)
    sb_ref[...] = jnp.zeros_like(sb_ref)
    row = lax.broadcasted_iota(jnp.int32, (c_, c_), 0)
    col = lax.broadcasted_iota(jnp.int32, (c_, c_), 1)
    causal = row >= col
    grow = lax.broadcasted_iota(jnp.int32, (r_, r_), 0)
    gcol = lax.broadcasted_iota(jnp.int32, (r_, r_), 1)
    shift = int(math.log2(c_))
    gtri = jnp.where((jnp.right_shift(grow, shift) == jnp.right_shift(gcol, shift)) & (grow >= gcol),
                     1.0, 0.0).astype(BF16)
    wa2 = wa2_ref[...]
    ba = ba_ref[...]
    gng = gng_ref[...]
    rng = rng_ref[...]
    rnb = rnb_ref[...]
    nt = (((1,), (1,)), ((), ()))
    tn = (((0,), (0,)), ((), ()))
    mm = functools.partial(jnp.dot, preferred_element_type=F32)
    mmg = functools.partial(lax.dot_general, preferred_element_type=F32)
    chunks = [slice(g * c_, (g + 1) * c_) for g in range(gsz)]
    hk = lambda x, hh: x[:, hh * DK:(hh + 1) * DK]
    hv = lambda x, hh: x[:, hh * DV:(hh + 1) * DV]
    per_head = lambda xs: jnp.concatenate(xs, axis=1)

    def group_rows(j):
        return pl.ds(pl.multiple_of(j * r_, c_), r_)

    def gate_logits(j):
        return mm(ga_ref[0, group_rows(j), :], wa2) + ba

    def log_decay(z, j):
        valid = j * r_ + lax.broadcasted_iota(jnp.int32, (r_, 1), 0) >= PAD
        log_sig = jnp.minimum(z, 0.0) - jnp.log(1.0 + jnp.exp(-jnp.abs(z)))
        la = jnp.where(valid, log_sig * (1.0 / GLA_TAU), 0.0)
        la_hi = la.astype(BF16)
        la_lo = (la - la_hi.astype(F32)).astype(BF16)
        return mm(gtri, la_hi) + mm(gtri, la_lo)

    def finish(j, slot):
        rows = group_rows(j)
        o = oa_ref[slot]
        on = []
        for hh in heads:
            o_h = hv(o, hh)
            ms = jnp.mean(o_h * o_h, axis=-1, keepdims=True)
            on.append(o_h * lax.rsqrt(ms + LN_EPS) * gng)
        ya_ref[0, rows, :] = _silu(gr_ref[0, rows, :]) * per_head(on).astype(BF16)
        ob = ob_ref[slot]
        obn = []
        for hh in heads:
            ob_h = hv(ob, hh)
            dd = ob_h - jnp.mean(ob_h, axis=-1, keepdims=True)
            var = jnp.mean(dd * dd, axis=-1, keepdims=True)
            obn.append(dd * lax.rsqrt(var + GN_EPS))
        yb_ref[0, rows, :] = _silu(rg_ref[0, rows, :]) * (per_head(obn) * rng + rnb).astype(BF16)

    oa_ref[...] = jnp.zeros_like(oa_ref)
    ob_ref[...] = jnp.zeros_like(ob_ref)
    b_ref[0] = log_decay(gate_logits(0), 0)

    def group(gi, carry):
        slot = lax.rem(gi, 2)
        nxt = jnp.minimum(gi + 1, n_groups - 1)
        z_next = gate_logits(nxt)

        rows = group_rows(gi)
        cos = cos_ref[rows, :]
        sin = sin_ref[rows, :]
        rq = rq_ref[0, rows, :].astype(F32)
        rk = rk_ref[0, rows, :].astype(F32)
        vb = rv_ref[0, rows, :]
        qr, s_b, st_b = [], [], []
        for hh in heads:
            rq_h, rk_h = hk(rq, hh), hk(rk, hh)
            qr.append(rq_h * cos + pltpu.roll(rq_h, DK // 2, 1) * sin)
            kr = (rk_h * cos + pltpu.roll(rk_h, DK // 2, 1) * sin) * (DK ** -0.5)
            qr_b, kr_b = qr[hh].astype(BF16), kr.astype(BF16)
            s_b.append([(mmg(qr_b[cs], kr_b[cs], nt) * dmat_ref[hh]).astype(BF16) for cs in chunks])
            st, prev = sb_ref[hh], []
            for cs in chunks:
                prev.append(st.astype(BF16))
                st = st * rdec_ref[hh] + mmg((kr[cs] * zeta_ref[hh]).astype(BF16), hv(vb, hh)[cs], tn)
            sb_ref[hh] = st
            st_b.append(prev)

        b = b_ref[slot]
        q = gq_ref[0, rows, :].astype(F32) * (DK ** -0.5)
        k = gk_ref[0, rows, :].astype(F32)
        v = gv_ref[0, rows, :]
        q_in = (q * jnp.exp(b)).astype(BF16)
        k_in = (k * jnp.exp(-b)).astype(BF16)
        s_a, st_a = [], []
        for hh in heads:
            s_a.append([jnp.where(causal, mmg(hk(q_in, hh)[cs], hk(k_in, hh)[cs], nt), 0.0).astype(BF16)
                        for cs in chunks])
            st, prev = sa_ref[hh], []
            for cs in chunks:
                b_c = hk(b, hh)[cs]
                b_last = b_c[c_ - 1:c_, :]
                k_tail = (hk(k, hh)[cs] * jnp.exp(b_last - b_c)).astype(BF16)
                dcol = jnp.transpose(jnp.broadcast_to(jnp.exp(b_last), (DK, DK)))
                prev.append(st.astype(BF16))
                st = st * jnp.concatenate([dcol, dcol], axis=1) + mmg(k_tail, hv(v, hh)[cs], tn)
            sa_ref[hh] = st
            st_a.append(prev)

        b_ref[1 - slot] = log_decay(z_next, nxt)
        finish(jnp.maximum(gi - 1, 0), 1 - slot)

        o_a, o_b = [], []
        for hh in heads:
            o_b.append(jnp.concatenate(
                [mm(s_b[hh][g], hv(vb, hh)[cs]) + mm((qr[hh][cs] * xi_ref[hh]).astype(BF16), st_b[hh][g])
                 for g, cs in enumerate(chunks)], axis=0))
        for hh in heads:
            o_a.append(jnp.concatenate(
                [mm(s_a[hh][g], hv(v, hh)[cs]) + mm(hk(q_in, hh)[cs], st_a[hh][g])
                 for g, cs in enumerate(chunks)], axis=0))
        oa_ref[slot] = per_head(o_a)
        ob_ref[slot] = per_head(o_b)
        return carry

    lax.fori_loop(0, n_groups, group, 0)
    finish(n_groups - 1, (n_groups - 1) % 2)


def _retention_tables(lp):
    half = DK // 2
    pos = np.arange(lp, dtype=np.float64) - PAD
    inv = ROPE_BASE ** (-np.arange(half, dtype=np.float64) / half)
    ang = pos[:, None] * inv[None, :]
    cos = np.concatenate([np.cos(ang), np.cos(ang)], axis=1)
    sin = np.concatenate([-np.sin(ang), np.sin(ang)], axis=1)
    lg = np.log(1.0 - 2.0 ** (-5.0 - np.arange(HEADS, dtype=np.float64)))
    idx = np.arange(CHUNK, dtype=np.float64)
    rel = idx[:, None] - idx[None, :]
    dmat = np.where(rel >= 0, np.exp(lg[:, None, None] * np.maximum(rel, 0.0)), 0.0)
    xi = np.broadcast_to(np.exp(lg[:, None] * (idx + 1.0))[:, :, None], (HEADS, CHUNK, DK))
    zeta = np.broadcast_to(np.exp(lg[:, None] * (CHUNK - 1.0 - idx))[:, :, None], (HEADS, CHUNK, DK))
    rdec = np.broadcast_to(np.exp(lg * CHUNK)[:, None, None], (HEADS, 1, DV))
    f = lambda a: jnp.asarray(np.ascontiguousarray(a), F32)
    return f(cos), f(sin), f(dmat), f(xi), f(zeta), f(rdec)


def _mixer(proj3, wa2, ba, gng, rng, rnb, tables):
    bsz, lp, _ = proj3.shape
    cos, sin, dmat, xi, zeta, rdec = tables
    n_chunks = lp // CHUNK
    gsz = CHUNKS_PER_TRIP if n_chunks % CHUNKS_PER_TRIP == 0 else 1
    hps = HEADS_PER_STEP
    kw, vw = hps * DK, hps * DV
    qk = lambda col: pl.BlockSpec((1, lp, kw), lambda b, h, c=col // kw: (b, 0, c + h))
    vv = lambda col: pl.BlockSpec((1, lp, vw), lambda b, h, c=col // vw: (b, 0, c + h))
    in_specs = [
        qk(COL_GQ), qk(COL_GK), vv(COL_GV), vv(COL_GR),
        pl.BlockSpec((1, lp, V7X_LANES), lambda b, h: (b, 0, COL_GA // V7X_LANES)),
        qk(COL_RQ), qk(COL_RK), vv(COL_RV), vv(COL_RG),
        pl.BlockSpec((V7X_LANES, kw), lambda b, h: (0, h)),
        pl.BlockSpec((1, kw), lambda b, h: (0, h)),
        pl.BlockSpec((1, DV), lambda b, h: (0, 0)),
        pl.BlockSpec((1, vw), lambda b, h: (0, h)),
        pl.BlockSpec((1, vw), lambda b, h: (0, h)),
        pl.BlockSpec((lp, DK), lambda b, h: (0, 0)),
        pl.BlockSpec((lp, DK), lambda b, h: (0, 0)),
        pl.BlockSpec((hps, CHUNK, CHUNK), lambda b, h: (h, 0, 0)),
        pl.BlockSpec((hps, CHUNK, DK), lambda b, h: (h, 0, 0)),
        pl.BlockSpec((hps, CHUNK, DK), lambda b, h: (h, 0, 0)),
        pl.BlockSpec((hps, 1, DV), lambda b, h: (h, 0, 0)),
    ]
    out_spec = pl.BlockSpec((1, lp, vw), lambda b, h: (b, 0, h))
    return pl.pallas_call(
        functools.partial(_mixer_body, n_groups=n_chunks // gsz, gsz=gsz, hps=hps),
        grid=(bsz, HEADS // hps),
        in_specs=in_specs,
        out_specs=[out_spec, out_spec],
        out_shape=[jax.ShapeDtypeStruct((bsz, lp, VW), BF16)] * 2,
        scratch_shapes=[pltpu.VMEM((hps, DK, DV), F32), pltpu.VMEM((hps, DK, DV), F32),
                        pltpu.VMEM((2, gsz * CHUNK, kw), F32),
                        pltpu.VMEM((2, gsz * CHUNK, vw), F32), pltpu.VMEM((2, gsz * CHUNK, vw), F32)],
        compiler_params=_params("arbitrary", "arbitrary"),
        name="mixer",
    )(*([proj3] * 4), proj3, *([proj3] * 4), wa2, ba, gng, rng, rnb, cos, sin, dmat, xi, zeta, rdec)


def _mix_out_body(ya_ref, yb_ref, gta_ref, gtb_ref, h_ref, wpa_ref, wpb_ref, wo_ref,
                  bga_ref, bgb_ref, g_ref, b_ref, o_ref, *rest, alpha, with_slabs):
    mg_ref = rest[-1]
    for c0 in range(0, mg_ref.shape[1], V7X_MXU_COLS):
        cols = slice(c0, c0 + V7X_MXU_COLS)
        y_a = jnp.dot(ya_ref[...], wpa_ref[0, :, cols], preferred_element_type=F32)
        y_b = jnp.dot(yb_ref[...], wpb_ref[0, :, cols], preferred_element_type=F32)
        mg_ref[:, cols] = (jax.nn.sigmoid(gta_ref[:, cols].astype(F32) + bga_ref[:, cols]) * y_a
                           + jax.nn.sigmoid(gtb_ref[:, cols].astype(F32) + bgb_ref[:, cols]) * y_b).astype(BF16)
    for rs in _row_blocks(mg_ref.shape[0]):
        mix = jnp.dot(mg_ref[rs, :], wo_ref[0], preferred_element_type=F32)
        out = _layer_norm(alpha * h_ref[rs, :] + mix, g_ref[...], b_ref[...])
        o_ref[rs, :] = out
        if with_slabs:
            _matrix_to_slabs(out, rest[0], rs.start)


def _mix_out(ya, yb, proj, h, wpa, wpb, wo, layer, bga, bgb, g, b, alpha, with_slabs):
    m, d = h.shape
    tm = _divisor_tile(m, 1056, V7X_BF16_SUBLANES)
    rows = lambda: pl.BlockSpec((tm, d), lambda i: (i, 0))
    out_specs = [rows()]
    out_shape = [jax.ShapeDtypeStruct((m, d), F32)]
    if with_slabs:
        assert d == SLAB * V7X_LANES
        out_specs.append(pl.BlockSpec((tm * SLAB, V7X_LANES), lambda i: (i, 0)))
        out_shape.append(jax.ShapeDtypeStruct((m * SLAB, V7X_LANES), F32))
    return pl.pallas_call(
        functools.partial(_mix_out_body, alpha=alpha, with_slabs=with_slabs),
        grid=(m // tm,),
        in_specs=[rows(), rows(),
                  pl.BlockSpec((tm, d), lambda i: (i, COL_GATE_A // d)),
                  pl.BlockSpec((tm, d), lambda i: (i, COL_GATE_B // d)),
                  rows(), _resident_layer((d, d), layer), _resident_layer((d, d), layer),
                  _resident_layer((d, d), layer),
                  _resident((1, d)), _resident((1, d)), _resident((1, d)), _resident((1, d))],
        out_specs=out_specs,
        out_shape=out_shape,
        scratch_shapes=[pltpu.VMEM((tm, d), BF16)],
        compiler_params=_params("arbitrary"),
        name="mix_out",
    )(ya, yb, proj, proj, h, wpa, wpb, wo, bga, bgb, g, b)


def _swiglu_hidden(xb, wg_ref, wu_ref, hid_ref, widx=()):
    dff = hid_ref.shape[1]
    for c0 in range(0, dff, V7X_MXU_COLS):
        cols = slice(c0, c0 + V7X_MXU_COLS)
        gate = jnp.dot(xb, wg_ref[widx + (slice(None), cols)], preferred_element_type=F32)
        up = jnp.dot(xb, wu_ref[widx + (slice(None), cols)], preferred_element_type=F32)
        hid_ref[:, cols] = (_silu(gate) * up).astype(BF16)


def _ffn_body(x_ref, wg_ref, wu_ref, wd_ref, g_ref, b_ref, o_ref, hid_ref, *, alpha):
    _swiglu_hidden(x_ref[...].astype(BF16), wg_ref, wu_ref, hid_ref, widx=(0,))
    for rs in _row_blocks(hid_ref.shape[0]):
        f = jnp.dot(hid_ref[rs, :], wd_ref[0], preferred_element_type=F32)
        o_ref[rs, :] = _layer_norm(alpha * x_ref[rs, :] + f, g_ref[...], b_ref[...])


def _ffn(h, wg, wu, wd, layer, g, b, alpha):
    m, d = h.shape
    dff = wg.shape[2]
    assert dff % V7X_MXU_COLS == 0
    tm = _divisor_tile(m, 1056, V7X_BF16_SUBLANES)
    return pl.pallas_call(
        functools.partial(_ffn_body, alpha=alpha),
        grid=(m // tm,),
        in_specs=[pl.BlockSpec((tm, d), lambda i: (i, 0)),
                  _resident_layer((d, dff), layer), _resident_layer((d, dff), layer),
                  _resident_layer((dff, d), layer), _resident((1, d)), _resident((1, d))],
        out_specs=pl.BlockSpec((tm, d), lambda i: (i, 0)),
        out_shape=jax.ShapeDtypeStruct((m, d), F32),
        scratch_shapes=[pltpu.VMEM((tm, dff), BF16)],
        compiler_params=_params("arbitrary"),
        name="ffn",
    )(h, wg, wu, wd, g, b)


def _router_body(h_ref, wr_ref, idx_ref, wt_ref, rank_ref, cnt_ref, tri_ref, run_ref):
    ne = N_EXPERTS
    tm = h_ref.shape[0]

    @pl.when(pl.program_id(0) == 0)
    def _():
        r = lax.broadcasted_iota(jnp.int32, (tm, tm), 0)
        c = lax.broadcasted_iota(jnp.int32, (tm, tm), 1)
        tri_ref[...] = jnp.where(r > c, 1.0, 0.0).astype(BF16)
        run_ref[...] = jnp.zeros_like(run_ref)

    h = h_ref[...]
    h_hi = h.astype(BF16)
    h_lo = (h - h_hi.astype(F32)).astype(BF16)
    w2 = wr_ref[...]
    both = jnp.dot(h_hi, w2, preferred_element_type=F32)
    logits = both[:, :ne] + both[:, ne:] + jnp.dot(h_lo, w2[:, :ne], preferred_element_type=F32)
    e = lax.broadcasted_iota(jnp.int32, logits.shape, 1)
    m1 = jnp.max(logits, axis=1, keepdims=True)
    i1 = jnp.min(jnp.where(logits == m1, e, ne), axis=1, keepdims=True)
    rest = jnp.where(e == i1, -jnp.inf, logits)
    m2 = jnp.max(rest, axis=1, keepdims=True)
    i2 = jnp.min(jnp.where(rest == m2, e, ne), axis=1, keepdims=True)
    e2 = jnp.exp(m2 - m1)
    w1 = 1.0 / (1.0 + e2)
    idx_ref[...] = jnp.concatenate([i1, i2], axis=1)
    wt_ref[...] = jnp.concatenate([w1, e2 * w1], axis=1)

    one1 = jnp.where(e == i1, 1.0, 0.0)
    one2 = jnp.where(e == i2, 1.0, 0.0)
    both_slots = one1 + one2
    before = jnp.dot(tri_ref[...], both_slots.astype(BF16), preferred_element_type=F32) + run_ref[...]
    r1 = jnp.sum(one1 * before, axis=1, keepdims=True)
    r2 = jnp.sum(one2 * before, axis=1, keepdims=True)
    rank_ref[...] = jnp.concatenate([r1, r2], axis=1).astype(jnp.int32)
    run_ref[...] += jnp.sum(both_slots, axis=0, keepdims=True)
    cnt_ref[...] = run_ref[...].astype(jnp.int32)


def _router(h, wr):
    m, d = h.shape
    tm = _divisor_tile(m, 1024, V7X_BF16_SUBLANES)
    w_hi = wr.astype(BF16)
    w2 = jnp.concatenate([w_hi, (wr - w_hi.astype(F32)).astype(BF16)], axis=1)
    per_token = lambda: pl.BlockSpec((tm, 2), lambda i: (i, 0))
    return pl.pallas_call(
        _router_body,
        grid=(m // tm,),
        in_specs=[pl.BlockSpec((tm, d), lambda i: (i, 0)), _resident((d, 2 * N_EXPERTS))],
        out_specs=[per_token(), per_token(), per_token(), pl.BlockSpec((1, N_EXPERTS), lambda i: (0, 0))],
        out_shape=[jax.ShapeDtypeStruct((m, 2), jnp.int32), jax.ShapeDtypeStruct((m, 2), F32),
                   jax.ShapeDtypeStruct((m, 2), jnp.int32), jax.ShapeDtypeStruct((1, N_EXPERTS), jnp.int32)],
        scratch_shapes=[pltpu.VMEM((tm, tm), BF16), pltpu.VMEM((1, N_EXPERTS), F32)],
        compiler_params=_params("arbitrary"),
        name="router",
    )(h, w2)


def _index_prefetch(idx_hbm, idx_smem, isem, i, n):
    width = idx_hbm.shape[1]
    slot = lax.rem(i, 2)
    half = lambda sl: idx_smem.at[pl.ds(pl.multiple_of(sl * width, width), width)]

    @pl.when(i == 0)
    def _():
        pltpu.make_async_copy(idx_hbm.at[0], half(0), isem.at[0]).start()

    pltpu.make_async_copy(idx_hbm.at[i], half(slot), isem.at[slot]).wait()

    @pl.when(i + 1 < n)
    def _():
        pltpu.make_async_copy(idx_hbm.at[i + 1], half(1 - slot), isem.at[1 - slot]).start()

    return slot * width


def _slab_at(off):
    return pl.ds(pl.multiple_of(off, SLAB), SLAB)


def _slab_rows(p):
    return pl.ds(pl.multiple_of(p * SLAB, SLAB), SLAB)


def _slabs_to_matrix(ref2, rows):
    return jnp.concatenate([ref2[pl.ds(g, rows, stride=SLAB), :] for g in range(SLAB)], axis=1)


def _matrix_to_slabs(x, ref2, row0=0):
    for g in range(SLAB):
        ref2[pl.ds(row0 * SLAB + g, x.shape[0], stride=SLAB), :] = x[:, g * V7X_LANES:(g + 1) * V7X_LANES]


def _dispatch_body(st_ref, cnt_ref, pd_ref, pos_hbm, h3_ref, xs_hbm, idx_smem, zbuf, isem, sem, psem,
                   *, tm, n_rows):
    i = pl.program_id(0)
    n = pl.num_programs(0)
    base = _index_prefetch(pos_hbm, idx_smem, isem, i, n)

    def issue(r, carry):
        src = h3_ref.at[_slab_rows(r)]
        pltpu.make_async_copy(src, xs_hbm.at[_slab_at(idx_smem[base + r])], sem).start()
        pltpu.make_async_copy(src, xs_hbm.at[_slab_at(idx_smem[base + tm + r])], sem).start(priority=1)
        return carry

    lax.fori_loop(0, tm, issue, 0, unroll=8)
    pltpu.make_async_copy(h3_ref, xs_hbm.at[pl.ds(0, tm * SLAB)], sem).wait()
    pltpu.make_async_copy(h3_ref, xs_hbm.at[pl.ds(0, tm * SLAB)], sem).wait()

    @pl.when(i == n - 1)
    def _():
        zbuf[...] = jnp.zeros_like(zbuf)
        zrow = zbuf.at[pl.ds(0, SLAB)]
        for e in range(N_EXPERTS):
            lo = st_ref[e] + cnt_ref[e]
            npad = pd_ref[e] - cnt_ref[e]

            def fill(r, carry):
                pltpu.make_async_copy(zrow, xs_hbm.at[_slab_rows(lo + r)], psem).start()
                return carry

            def drain(r, carry):
                pltpu.make_async_copy(zrow, xs_hbm.at[_slab_rows(lo + r)], psem).wait()
                return carry

            lax.fori_loop(0, npad, fill, 0)
            lax.fori_loop(0, npad, drain, 0)

        zr = zbuf.shape[0] // SLAB
        tail = st_ref[N_EXPERTS - 1] + pd_ref[N_EXPERTS - 1]
        nblk = (n_rows - tail) // zr

        def block(k):
            return xs_hbm.at[pl.ds(pl.multiple_of((tail + k * zr) * SLAB, SLAB), zr * SLAB)]

        def fill_block(k, carry):
            pltpu.make_async_copy(zbuf, block(k), psem).start()
            return carry

        def drain_block(k, carry):
            pltpu.make_async_copy(zbuf, block(k), psem).wait()
            return carry

        lax.fori_loop(0, nblk, fill_block, 0)
        lax.fori_loop(0, nblk, drain_block, 0)


def _dispatch(h3, pos2d, starts, counts, padded, n_rows, tm):
    m = h3.shape[0] // SLAB
    grid_spec = pltpu.PrefetchScalarGridSpec(
        num_scalar_prefetch=3,
        grid=(m // tm,),
        in_specs=[pl.BlockSpec(memory_space=pl.ANY),
                  pl.BlockSpec((tm * SLAB, V7X_LANES), lambda i, *_: (i, 0))],
        out_specs=pl.BlockSpec(memory_space=pl.ANY),
        scratch_shapes=[pltpu.SMEM((4 * tm,), jnp.int32), pltpu.VMEM((ZERO_ROWS * SLAB, V7X_LANES), F32),
                        pltpu.SemaphoreType.DMA((2,)), pltpu.SemaphoreType.DMA, pltpu.SemaphoreType.DMA],
    )
    return pl.pallas_call(
        functools.partial(_dispatch_body, tm=tm, n_rows=n_rows),
        grid_spec=grid_spec,
        out_shape=jax.ShapeDtypeStruct((n_rows * SLAB, V7X_LANES), F32),
        compiler_params=_params("arbitrary"),
        name="moe_dispatch",
    )(starts, counts, padded, pos2d, h3)


def _moe_body(te_ref, nu_ref, x_ref, wg_ref, wu_ref, wd_ref, o_ref, hid_ref):
    @pl.when(pl.program_id(0) < nu_ref[0])
    def _():
        xb = _slabs_to_matrix(x_ref, hid_ref.shape[0]).astype(BF16)
        _swiglu_hidden(xb, wg_ref, wu_ref, hid_ref, widx=(0, 0))
        for rs in _row_blocks(hid_ref.shape[0]):
            _matrix_to_slabs(jnp.dot(hid_ref[rs, :], wd_ref[0, 0], preferred_element_type=F32), o_ref, rs.start)

    @pl.when(pl.program_id(0) >= nu_ref[0])
    def _():
        o_ref[...] = jnp.zeros_like(o_ref)


def _moe_experts(xs, tile_expert, n_used, wg, wu, wd, layer, tme):
    r = xs.shape[0] // SLAB
    d, dff = wg.shape[2], wg.shape[3]
    assert dff % V7X_MXU_COLS == 0
    expert = lambda shape: pl.BlockSpec((1, 1) + shape, lambda i, te, nu: (layer, te[i], 0, 0),
                                        pipeline_mode=pl.Buffered(1))
    grid_spec = pltpu.PrefetchScalarGridSpec(
        num_scalar_prefetch=2,
        grid=(r // tme,),
        in_specs=[pl.BlockSpec((tme * SLAB, V7X_LANES), lambda i, te, nu: (i, 0)),
                  expert((d, dff)), expert((d, dff)), expert((dff, d))],
        out_specs=pl.BlockSpec((tme * SLAB, V7X_LANES), lambda i, te, nu: (i, 0)),
        scratch_shapes=[pltpu.VMEM((tme, dff), BF16)],
    )
    return pl.pallas_call(
        _moe_body,
        grid_spec=grid_spec,
        out_shape=jax.ShapeDtypeStruct(xs.shape, F32),
        compiler_params=_params("arbitrary"),
        name="moe_experts",
    )(tile_expert, n_used, xs, wg, wu, wd)


def _combine_body(pos_hbm, ys_hbm, wt_ref, h_ref, g_ref, b_ref, o_ref, ybuf, idx_smem, isem, sem, *, tm, alpha):
    i = pl.program_id(0)
    base = _index_prefetch(pos_hbm, idx_smem, isem, i, pl.num_programs(0))

    def issue(r, carry):
        pltpu.make_async_copy(ys_hbm.at[_slab_at(idx_smem[base + r])], ybuf.at[0, _slab_rows(r)], sem).start()
        pltpu.make_async_copy(ys_hbm.at[_slab_at(idx_smem[base + tm + r])], ybuf.at[1, _slab_rows(r)],
                              sem).start(priority=1)
        return carry

    lax.fori_loop(0, tm, issue, 0, unroll=8)
    pltpu.make_async_copy(ys_hbm.at[pl.ds(0, tm * SLAB)], ybuf.at[0], sem).wait()
    pltpu.make_async_copy(ys_hbm.at[pl.ds(0, tm * SLAB)], ybuf.at[1], sem).wait()
    for rs in _row_blocks(tm):
        n = rs.stop - rs.start
        rows = lambda k: ybuf.at[k, pl.ds(rs.start * SLAB, n * SLAB)]
        wt = wt_ref[rs, :]
        out = wt[:, 0:1] * _slabs_to_matrix(rows(0), n) + wt[:, 1:2] * _slabs_to_matrix(rows(1), n)
        o_ref[rs, :] = _layer_norm(alpha * h_ref[rs, :] + out, g_ref[...], b_ref[...])


def _combine(pos2d, ys, wt, h, g, b, alpha, tm):
    m, d = h.shape
    return pl.pallas_call(
        functools.partial(_combine_body, tm=tm, alpha=alpha),
        grid=(m // tm,),
        in_specs=[pl.BlockSpec(memory_space=pl.ANY), pl.BlockSpec(memory_space=pl.ANY),
                  pl.BlockSpec((tm, 2), lambda i: (i, 0)),
                  pl.BlockSpec((tm, d), lambda i: (i, 0)),
                  pl.BlockSpec((1, d), lambda i: (0, 0)), pl.BlockSpec((1, d), lambda i: (0, 0))],
        out_specs=pl.BlockSpec((tm, d), lambda i: (i, 0)),
        out_shape=jax.ShapeDtypeStruct((m, d), F32),
        scratch_shapes=[pltpu.VMEM((2, tm * SLAB, V7X_LANES), F32), pltpu.SMEM((4 * tm,), jnp.int32),
                        pltpu.SemaphoreType.DMA((2,)), pltpu.SemaphoreType.DMA],
        compiler_params=_params("arbitrary"),
        name="moe_combine",
    )(pos2d, ys, wt, h, g, b)


def _moe_layer(h, h_slabs, wr, wg, wu, wd, layer, g, b, alpha):
    m, d = h.shape
    assert d == SLAB * V7X_LANES
    tme = _divisor_tile(m, 1024, ZERO_ROWS)
    tmc = _divisor_tile(m, 1024, 8)
    idx, wts, rank, counts = _router(h, wr)

    counts = counts[0]
    padded = ((counts + tme - 1) // tme) * tme
    ends = jnp.cumsum(padded)
    starts = ends - padded
    pos = jnp.sum(jnp.where(idx[:, :, None] == jnp.arange(N_EXPERTS, dtype=jnp.int32), starts, 0), axis=-1) + rank
    n_rows = 2 * m + N_EXPERTS * tme
    n_tiles = n_rows // tme
    tile_start = jnp.arange(n_tiles, dtype=jnp.int32) * tme
    tile_expert = jnp.minimum(jnp.sum(tile_start[:, None] >= ends[None, :], axis=1), N_EXPERTS - 1).astype(jnp.int32)
    n_used = (ends[-1] // tme).astype(jnp.int32).reshape(1)
    pos2d = (pos * SLAB).reshape(m // tmc, tmc, 2).transpose(0, 2, 1).reshape(m // tmc, 2 * tmc)

    xs = _dispatch(h_slabs, pos2d, starts, counts, padded, n_rows, tmc)
    ys = _moe_experts(xs, tile_expert, n_used, wg, wu, wd, layer, tme)
    return _combine(pos2d, ys, wts, h, g, b, alpha, tmc)


def kernel(x, meta, w_in, b_gate, gla_w_a2, gla_b_a, gla_norm_g, ret_norm_g, ret_norm_b, w_pa, w_pb, w_o,
           ln1_g, ln1_b, ffn_w_gate, ffn_w_up, ffn_w_down, moe_w_router, moe_w_gate, moe_w_up, moe_w_down,
           ln2_g, ln2_b):
    bsz, seq, d = x.shape
    depth = w_in.shape[0]
    alpha = (2 * depth) ** 0.25
    lp = seq + N_META + PAD
    assert lp % CHUNK == 0 and d == VW
    m = bsz * lp

    h = jnp.concatenate([jnp.zeros((bsz, PAD, d), x.dtype),
                         jnp.broadcast_to(meta[None].astype(x.dtype), (bsz, N_META, d)), x], axis=1)
    h = h.reshape(m, d)
    tables = _retention_tables(lp)
    keep = jnp.asarray(np.tile((np.arange(lp) >= PAD).astype(np.float32), bsz).reshape(m, 1))

    c0 = 2 * QK + 2 * VW
    row = lambda a: a.reshape(1, -1)
    moe_wg, moe_wu, moe_wd = moe_w_gate.astype(BF16), moe_w_up.astype(BF16), moe_w_down.astype(BF16)
    ffn_wg, ffn_wu, ffn_wd = ffn_w_gate.astype(BF16), ffn_w_up.astype(BF16), ffn_w_down.astype(BF16)
    wpa_all, wpb_all, wo_all = w_pa.astype(BF16), w_pb.astype(BF16), w_o.astype(BF16)
    for l in range(depth):
        w = w_in[l]
        w_perm = jnp.concatenate(
            [w[:, :c0], w[:, c0 + GLA_RANK:], w[:, c0:c0 + GLA_RANK],
             jnp.zeros((d, V7X_MXU_COLS - GLA_RANK), w.dtype)], axis=1).astype(BF16)
        wa2 = jnp.concatenate([gla_w_a2[l], jnp.zeros((V7X_LANES - GLA_RANK, QK), F32)], axis=0).astype(BF16)

        proj = _in_proj(h, keep, w_perm)
        ya, yb = _mixer(proj.reshape(bsz, lp, N_PROJ), wa2, row(gla_b_a[l]), row(gla_norm_g[l]),
                        row(ret_norm_g[l]), row(ret_norm_b[l]), tables)
        is_moe = l % 2 == 1
        mixed = _mix_out(ya.reshape(m, VW), yb.reshape(m, VW), proj, h,
                         wpa_all, wpb_all, wo_all, l,
                         row(b_gate[l, 0]), row(b_gate[l, 1]), row(ln1_g[l]), row(ln1_b[l]), alpha, is_moe)
        h = mixed[0]
        if l % 2 == 0:
            j = l // 2
            h = _ffn(h, ffn_wg, ffn_wu, ffn_wd, j, row(ln2_g[l]), row(ln2_b[l]), alpha)
        else:
            j = l // 2
            h = _moe_layer(h, mixed[1], moe_w_router[j], moe_wg, moe_wu, moe_wd, j,
                           row(ln2_g[l]), row(ln2_b[l]), alpha)
    return h.reshape(bsz, lp, d)[:, PAD + N_META:]
```

```python
import functools
import math

import jax
import jax.numpy as jnp
import numpy as np
from jax import lax
from jax.experimental import pallas as pl
from jax.experimental.pallas import tpu as pltpu

F32 = jnp.float32
BF16 = jnp.bfloat16

N_META = 16
CHUNK = 64
PAD = CHUNK - N_META
HEADS = 4
HEADS_PER_STEP = 2
CHUNKS_PER_TRIP = 3
DK = 128
DV = 256
GLA_RANK = 16
GLA_TAU = 16.0
ROPE_BASE = 10000.0
N_EXPERTS = 8
LN_EPS = 1e-5
GN_EPS = 1e-6

V7X_LANES = 128
V7X_BF16_SUBLANES = 16
V7X_MXU_COLS = 256
V7X_VMEM_LIMIT_BYTES = 56 * 1024 * 1024

SLAB = 8
ZERO_ROWS = 128

QK = HEADS * DK
VW = HEADS * DV
COL_GQ, COL_GK, COL_GV, COL_GR = 0, QK, 2 * QK, 2 * QK + VW
COL_RQ = 2 * QK + 2 * VW
COL_RK, COL_RV, COL_RG = COL_RQ + QK, COL_RQ + 2 * QK, COL_RQ + 2 * QK + VW
COL_GATE_A = COL_RQ + 2 * QK + 2 * VW
COL_GATE_B = COL_GATE_A + QK + QK
COL_GA = COL_GATE_B + QK + QK
N_PROJ = COL_GA + V7X_MXU_COLS


def _divisor_tile(n, cap, multiple):
    best = None
    for t in range(multiple, min(n, cap) + 1, multiple):
        if n % t == 0:
            best = t
    assert best is not None, (n, cap, multiple)
    return best


def _params(*sem):
    return pltpu.CompilerParams(dimension_semantics=sem, vmem_limit_bytes=V7X_VMEM_LIMIT_BYTES)


def _resident(shape):
    return pl.BlockSpec(shape, lambda *_: (0,) * len(shape), pipeline_mode=pl.Buffered(1))


def _resident_layer(shape, layer):
    return pl.BlockSpec((1,) + shape, lambda *_: (layer,) + (0,) * len(shape), pipeline_mode=pl.Buffered(1))


def _row_blocks(rows):
    for n in (4, 3, 2):
        if rows % n == 0 and (rows // n) % V7X_BF16_SUBLANES == 0:
            return [slice(k * rows // n, (k + 1) * rows // n) for k in range(n)]
    return [slice(0, rows)]


def _layer_norm(r, g, b):
    mu = jnp.mean(r, axis=-1, keepdims=True)
    d = r - mu
    var = jnp.mean(d * d, axis=-1, keepdims=True)
    return d * lax.rsqrt(var + LN_EPS) * g + b


def _silu(x):
    return x * jax.nn.sigmoid(x)


def _in_proj_body(x_ref, keep_ref, w_ref, o_ref, xb_ref):
    @pl.when(pl.program_id(1) == 0)
    def _():
        xb_ref[...] = jnp.where(keep_ref[...] > 0.0, x_ref[...], 0.0).astype(BF16)

    xb = xb_ref[...]
    for c0 in range(0, o_ref.shape[1], V7X_MXU_COLS):
        cols = slice(c0, c0 + V7X_MXU_COLS)
        o_ref[:, cols] = jnp.dot(xb, w_ref[0, :, cols], preferred_element_type=F32).astype(BF16)


def _in_proj(h, keep, w, layer):
    m, d = h.shape
    n = w.shape[2]
    tm = _divisor_tile(m, 1056, V7X_BF16_SUBLANES)
    tn = _divisor_tile(n, 2816, V7X_MXU_COLS)
    return pl.pallas_call(
        _in_proj_body,
        grid=(m // tm, n // tn),
        in_specs=[pl.BlockSpec((tm, d), lambda i, j: (i, 0)), pl.BlockSpec((tm, 1), lambda i, j: (i, 0)),
                  pl.BlockSpec((1, d, tn), lambda i, j: (layer, 0, j))],
        out_specs=pl.BlockSpec((tm, tn), lambda i, j: (i, j)),
        out_shape=jax.ShapeDtypeStruct((m, n), BF16),
        scratch_shapes=[pltpu.VMEM((tm, d), BF16)],
        compiler_params=_params("arbitrary", "arbitrary"),
        name="in_proj",
    )(h, keep, w)


def _mixer_body(gq_ref, gk_ref, gv_ref, gr_ref, ga_ref, rq_ref, rk_ref, rv_ref, rg_ref,
                wa2_ref, ba_ref, gng_ref, rng_ref, rnb_ref, cos_ref, sin_ref,
                dmat_ref, xi_ref, zeta_ref, rdec_ref,
                ya_ref, yb_ref, sa_ref, sb_ref, b_ref, oa_ref, ob_ref, *, n_groups, gsz, hps):
    c_ = CHUNK
    r_ = gsz * c_
    heads = range(hps)
    sa_ref[...] = jnp.zeros_like(sa_ref)
    sb_ref[...] = jnp.zeros_like(sb_ref)
    row = lax.broadcasted_iota(jnp.int32, (c_, c_), 0)
    col = lax.broadcasted_iota(jnp.int32, (c_, c_), 1)
    causal = row >= col
    grow = lax.broadcasted_iota(jnp.int32, (r_, r_), 0)
    gcol = lax.broadcasted_iota(jnp.int32, (r_, r_), 1)
    shift = int(math.log2(c_))
    gtri = jnp.where((jnp.right_shift(grow, shift) == jnp.right_shift(gcol, shift)) & (grow >= gcol),
                     1.0, 0.0).astype(BF16)
    wa2 = wa2_ref[...]
    ba = ba_ref[...]
    gng = gng_ref[...]
    rng = rng_ref[...]
    rnb = rnb_ref[...]
    nt = (((1,), (1,)), ((), ()))
    tn = (((0,), (0,)), ((), ()))
    mm = functools.partial(jnp.dot, preferred_element_type=F32)
    mmg = functools.partial(lax.dot_general, preferred_element_type=F32)
    chunks = [slice(g * c_, (g + 1) * c_) for g in range(gsz)]
    hk = lambda x, hh: x[:, hh * DK:(hh + 1) * DK]
    hv = lambda x, hh: x[:, hh * DV:(hh + 1) * DV]
    per_head = lambda xs: jnp.concatenate(xs, axis=1)

    def group_rows(j):
        return pl.ds(pl.multiple_of(j * r_, c_), r_)

    def gate_logits(j):
        return mm(ga_ref[0, group_rows(j), :], wa2) + ba

    def log_decay(z, j):
        valid = j * r_ + lax.broadcasted_iota(jnp.int32, (r_, 1), 0) >= PAD
        log_sig = jnp.minimum(z, 0.0) - jnp.log(1.0 + jnp.exp(-jnp.abs(z)))
        la = jnp.where(valid, log_sig * (1.0 / GLA_TAU), 0.0)
        la_hi = la.astype(BF16)
        la_lo = (la - la_hi.astype(F32)).astype(BF16)
        return mm(gtri, la_hi) + mm(gtri, la_lo)

    def finish(j, slot):
        rows = group_rows(j)
        o = oa_ref[slot]
        on = []
        for hh in heads:
            o_h = hv(o, hh)
            ms = jnp.mean(o_h * o_h, axis=-1, keepdims=True)
            on.append(o_h * lax.rsqrt(ms + LN_EPS) * gng)
        ya_ref[0, rows, :] = _silu(gr_ref[0, rows, :]) * per_head(on).astype(BF16)
        ob = ob_ref[slot]
        obn = []
        for hh in heads:
            ob_h = hv(ob, hh)
            dd = ob_h - jnp.mean(ob_h, axis=-1, keepdims=True)
            var = jnp.mean(dd * dd, axis=-1, keepdims=True)
            obn.append(dd * lax.rsqrt(var + GN_EPS))
        yb_ref[0, rows, :] = _silu(rg_ref[0, rows, :]) * (per_head(obn) * rng + rnb).astype(BF16)

    oa_ref[...] = jnp.zeros_like(oa_ref)
    ob_ref[...] = jnp.zeros_like(ob_ref)
    b_ref[0] = log_decay(gate_logits(0), 0)

    def group(gi, carry):
        slot = lax.rem(gi, 2)
        nxt = jnp.minimum(gi + 1, n_groups - 1)
        z_next = gate_logits(nxt)

        rows = group_rows(gi)
        cos = cos_ref[rows, :]
        sin = sin_ref[rows, :]
        rq = rq_ref[0, rows, :].astype(F32)
        rk = rk_ref[0, rows, :].astype(F32)
        vb = rv_ref[0, rows, :]
        qr, s_b, st_b = [], [], []
        for hh in heads:
            rq_h, rk_h = hk(rq, hh), hk(rk, hh)
            qr.append(rq_h * cos + pltpu.roll(rq_h, DK // 2, 1) * sin)
            kr = (rk_h * cos + pltpu.roll(rk_h, DK // 2, 1) * sin) * (DK ** -0.5)
            qr_b, kr_b = qr[hh].astype(BF16), kr.astype(BF16)
            s_b.append([(mmg(qr_b[cs], kr_b[cs], nt) * dmat_ref[hh]).astype(BF16) for cs in chunks])
            st, prev = sb_ref[hh], []
            for cs in chunks:
                prev.append(st.astype(BF16))
                st = st * rdec_ref[hh] + mmg((kr[cs] * zeta_ref[hh]).astype(BF16), hv(vb, hh)[cs], tn)
            sb_ref[hh] = st
            st_b.append(prev)

        b = b_ref[slot]
        q = gq_ref[0, rows, :].astype(F32) * (DK ** -0.5)
        k = gk_ref[0, rows, :].astype(F32)
        v = gv_ref[0, rows, :]
        q_in = (q * jnp.exp(b)).astype(BF16)
        k_in = (k * jnp.exp(-b)).astype(BF16)
        s_a, st_a = [], []
        for hh in heads:
            s_a.append([jnp.where(causal, mmg(hk(q_in, hh)[cs], hk(k_in, hh)[cs], nt), 0.0).astype(BF16)
                        for cs in chunks])
            st, prev = sa_ref[hh], []
            for cs in chunks:
                b_c = hk(b, hh)[cs]
                b_last = b_c[c_ - 1:c_, :]
                k_tail = (hk(k, hh)[cs] * jnp.exp(b_last - b_c)).astype(BF16)
                dcol = jnp.transpose(jnp.broadcast_to(jnp.exp(b_last), (DK, DK)))
                prev.append(st.astype(BF16))
                st = st * jnp.concatenate([dcol, dcol], axis=1) + mmg(k_tail, hv(v, hh)[cs], tn)
            sa_ref[hh] = st
            st_a.append(prev)

        b_ref[1 - slot] = log_decay(z_next, nxt)
        finish(jnp.maximum(gi - 1, 0), 1 - slot)

        o_a, o_b = [], []
        for hh in heads:
            o_b.append(jnp.concatenate(
                [mm(s_b[hh][g], hv(vb, hh)[cs]) + mm((qr[hh][cs] * xi_ref[hh]).astype(BF16), st_b[hh][g])
                 for g, cs in enumerate(chunks)], axis=0))
        for hh in heads:
            o_a.append(jnp.concatenate(
                [mm(s_a[hh][g], hv(v, hh)[cs]) + mm(hk(q_in, hh)[cs], st_a[hh][g])
                 for g, cs in enumerate(chunks)], axis=0))
        oa_ref[slot] = per_head(o_a)
        ob_ref[slot] = per_head(o_b)
        return carry

    lax.fori_loop(0, n_groups, group, 0)
    finish(n_groups - 1, (n_groups - 1) % 2)


def _retention_tables(lp):
    half = DK // 2
    pos = np.arange(lp, dtype=np.float64) - PAD
    inv = ROPE_BASE ** (-np.arange(half, dtype=np.float64) / half)
    ang = pos[:, None] * inv[None, :]
    cos = np.concatenate([np.cos(ang), np.cos(ang)], axis=1)
    sin = np.concatenate([-np.sin(ang), np.sin(ang)], axis=1)
    lg = np.log(1.0 - 2.0 ** (-5.0 - np.arange(HEADS, dtype=np.float64)))
    idx = np.arange(CHUNK, dtype=np.float64)
    rel = idx[:, None] - idx[None, :]
    dmat = np.where(rel >= 0, np.exp(lg[:, None, None] * np.maximum(rel, 0.0)), 0.0)
    xi = np.broadcast_to(np.exp(lg[:, None] * (idx + 1.0))[:, :, None], (HEADS, CHUNK, DK))
    zeta = np.broadcast_to(np.exp(lg[:, None] * (CHUNK - 1.0 - idx))[:, :, None], (HEADS, CHUNK, DK))
    rdec = np.broadcast_to(np.exp(lg * CHUNK)[:, None, None], (HEADS, 1, DV))
    f = lambda a: jnp.asarray(np.ascontiguousarray(a), F32)
    return f(cos), f(sin), f(dmat), f(xi), f(zeta), f(rdec)


def _mixer(proj3, wa2, ba, gng, rng, rnb, tables):
    bsz, lp, _ = proj3.shape
    cos, sin, dmat, xi, zeta, rdec = tables
    n_chunks = lp // CHUNK
    gsz = CHUNKS_PER_TRIP if n_chunks % CHUNKS_PER_TRIP == 0 else 1
    hps = HEADS_PER_STEP
    kw, vw = hps * DK, hps * DV
    qk = lambda col: pl.BlockSpec((1, lp, kw), lambda b, h, c=col // kw: (b, 0, c + h))
    vv = lambda col: pl.BlockSpec((1, lp, vw), lambda b, h, c=col // vw: (b, 0, c + h))
    in_specs = [
        qk(COL_GQ), qk(COL_GK), vv(COL_GV), vv(COL_GR),
        pl.BlockSpec((1, lp, V7X_LANES), lambda b, h: (b, 0, COL_GA // V7X_LANES)),
        qk(COL_RQ), qk(COL_RK), vv(COL_RV), vv(COL_RG),
        pl.BlockSpec((V7X_LANES, kw), lambda b, h: (0, h)),
        pl.BlockSpec((1, kw), lambda b, h: (0, h)),
        pl.BlockSpec((1, DV), lambda b, h: (0, 0)),
        pl.BlockSpec((1, vw), lambda b, h: (0, h)),
        pl.BlockSpec((1, vw), lambda b, h: (0, h)),
        pl.BlockSpec((lp, DK), lambda b, h: (0, 0)),
        pl.BlockSpec((lp, DK), lambda b, h: (0, 0)),
        pl.BlockSpec((hps, CHUNK, CHUNK), lambda b, h: (h, 0, 0)),
        pl.BlockSpec((hps, CHUNK, DK), lambda b, h: (h, 0, 0)),
        pl.BlockSpec((hps, CHUNK, DK), lambda b, h: (h, 0, 0)),
        pl.BlockSpec((hps, 1, DV), lambda b, h: (h, 0, 0)),
    ]
    out_spec = pl.BlockSpec((1, lp, vw), lambda b, h: (b, 0, h))
    return pl.pallas_call(
        functools.partial(_mixer_body, n_groups=n_chunks // gsz, gsz=gsz, hps=hps),
        grid=(bsz, HEADS // hps),
        in_specs=in_specs,
        out_specs=[out_spec, out_spec],
        out_shape=[jax.ShapeDtypeStruct((bsz, lp, VW), BF16)] * 2,
        scratch_shapes=[pltpu.VMEM((hps, DK, DV), F32), pltpu.VMEM((hps, DK, DV), F32),
                        pltpu.VMEM((2, gsz * CHUNK, kw), F32),
                        pltpu.VMEM((2, gsz * CHUNK, vw), F32), pltpu.VMEM((2, gsz * CHUNK, vw), F32)],
        compiler_params=_params("arbitrary", "arbitrary"),
        name="mixer",
    )(*([proj3] * 4), proj3, *([proj3] * 4), wa2, ba, gng, rng, rnb, cos, sin, dmat, xi, zeta, rdec)


def _mix_out_body(ya_ref, yb_ref, gta_ref, gtb_ref, h_ref, wpa_ref, wpb_ref, wo_ref,
                  bga_ref, bgb_ref, g_ref, b_ref, o_ref, *rest, alpha, with_slabs):
    mg_ref = rest[-1]
    for c0 in range(0, mg_ref.shape[1], V7X_MXU_COLS):
        cols = slice(c0, c0 + V7X_MXU_COLS)
        y_a = jnp.dot(ya_ref[...], wpa_ref[0, :, cols], preferred_element_type=F32)
        y_b = jnp.dot(yb_ref[...], wpb_ref[0, :, cols], preferred_element_type=F32)
        mg_ref[:, cols] = (jax.nn.sigmoid(gta_ref[:, cols].astype(F32) + bga_ref[:, cols]) * y_a
                           + jax.nn.sigmoid(gtb_ref[:, cols].astype(F32) + bgb_ref[:, cols]) * y_b).astype(BF16)
    for rs in _row_blocks(mg_ref.shape[0]):
        mix = jnp.dot(mg_ref[rs, :], wo_ref[0], preferred_element_type=F32)
        out = _layer_norm(alpha * h_ref[rs, :] + mix, g_ref[...], b_ref[...])
        o_ref[rs, :] = out
        if with_slabs:
            _matrix_to_slabs(out, rest[0], rs.start)


def _mix_out(ya, yb, proj, h, wpa, wpb, wo, layer, bga, bgb, g, b, alpha, with_slabs):
    m, d = h.shape
    tm = _divisor_tile(m, 1056, V7X_BF16_SUBLANES)
    rows = lambda: pl.BlockSpec((tm, d), lambda i: (i, 0))
    out_specs = [rows()]
    out_shape = [jax.ShapeDtypeStruct((m, d), F32)]
    if with_slabs:
        assert d == SLAB * V7X_LANES
        out_specs.append(pl.BlockSpec((tm * SLAB, V7X_LANES), lambda i: (i, 0)))
        out_shape.append(jax.ShapeDtypeStruct((m * SLAB, V7X_LANES), F32))
    return pl.pallas_call(
        functools.partial(_mix_out_body, alpha=alpha, with_slabs=with_slabs),
        grid=(m // tm,),
        in_specs=[rows(), rows(),
                  pl.BlockSpec((tm, d), lambda i: (i, COL_GATE_A // d)),
                  pl.BlockSpec((tm, d), lambda i: (i, COL_GATE_B // d)),
                  rows(), _resident_layer((d, d), layer), _resident_layer((d, d), layer),
                  _resident_layer((d, d), layer),
                  _resident((1, d)), _resident((1, d)), _resident((1, d)), _resident((1, d))],
        out_specs=out_specs,
        out_shape=out_shape,
        scratch_shapes=[pltpu.VMEM((tm, d), BF16)],
        compiler_params=_params("arbitrary"),
        name="mix_out",
    )(ya, yb, proj, proj, h, wpa, wpb, wo, bga, bgb, g, b)


def _swiglu_hidden(xb, wg_ref, wu_ref, hid_ref, widx=()):
    dff = hid_ref.shape[1]
    for c0 in range(0, dff, V7X_MXU_COLS):
        cols = slice(c0, c0 + V7X_MXU_COLS)
        gate = jnp.dot(xb, wg_ref[widx + (slice(None), cols)], preferred_element_type=F32)
        up = jnp.dot(xb, wu_ref[widx + (slice(None), cols)], preferred_element_type=F32)
        hid_ref[:, cols] = (_silu(gate) * up).astype(BF16)


def _ffn_body(x_ref, wg_ref, wu_ref, wd_ref, g_ref, b_ref, o_ref, hid_ref, *, alpha):
    _swiglu_hidden(x_ref[...].astype(BF16), wg_ref, wu_ref, hid_ref, widx=(0,))
    for rs in _row_blocks(hid_ref.shape[0]):
        f = jnp.dot(hid_ref[rs, :], wd_ref[0], preferred_element_type=F32)
        o_ref[rs, :] = _layer_norm(alpha * x_ref[rs, :] + f, g_ref[...], b_ref[...])


def _ffn(h, wg, wu, wd, layer, g, b, alpha):
    m, d = h.shape
    dff = wg.shape[2]
    assert dff % V7X_MXU_COLS == 0
    tm = _divisor_tile(m, 1056, V7X_BF16_SUBLANES)
    return pl.pallas_call(
        functools.partial(_ffn_body, alpha=alpha),
        grid=(m // tm,),
        in_specs=[pl.BlockSpec((tm, d), lambda i: (i, 0)),
                  _resident_layer((d, dff), layer), _resident_layer((d, dff), layer),
                  _resident_layer((dff, d), layer), _resident((1, d)), _resident((1, d))],
        out_specs=pl.BlockSpec((tm, d), lambda i: (i, 0)),
        out_shape=jax.ShapeDtypeStruct((m, d), F32),
        scratch_shapes=[pltpu.VMEM((tm, dff), BF16)],
        compiler_params=_params("arbitrary"),
        name="ffn",
    )(h, wg, wu, wd, g, b)


def _router_body(h_ref, wr_ref, idx_ref, wt_ref, rank_ref, cnt_ref, tri_ref, run_ref):
    ne = N_EXPERTS
    tm = h_ref.shape[0]

    @pl.when(pl.program_id(0) == 0)
    def _():
        r = lax.broadcasted_iota(jnp.int32, (tm, tm), 0)
        c = lax.broadcasted_iota(jnp.int32, (tm, tm), 1)
        tri_ref[...] = jnp.where(r > c, 1.0, 0.0).astype(BF16)
        run_ref[...] = jnp.zeros_like(run_ref)

    h = h_ref[...]
    h_hi = h.astype(BF16)
    h_lo = (h - h_hi.astype(F32)).astype(BF16)
    w2 = wr_ref[...]
    both = jnp.dot(h_hi, w2, preferred_element_type=F32)
    logits = both[:, :ne] + both[:, ne:] + jnp.dot(h_lo, w2[:, :ne], preferred_element_type=F32)
    e = lax.broadcasted_iota(jnp.int32, logits.shape, 1)
    m1 = jnp.max(logits, axis=1, keepdims=True)
    i1 = jnp.min(jnp.where(logits == m1, e, ne), axis=1, keepdims=True)
    rest = jnp.where(e == i1, -jnp.inf, logits)
    m2 = jnp.max(rest, axis=1, keepdims=True)
    i2 = jnp.min(jnp.where(rest == m2, e, ne), axis=1, keepdims=True)
    e2 = jnp.exp(m2 - m1)
    w1 = 1.0 / (1.0 + e2)
    idx_ref[...] = jnp.concatenate([i1, i2], axis=1)
    wt_ref[...] = jnp.concatenate([w1, e2 * w1], axis=1)

    one1 = jnp.where(e == i1, 1.0, 0.0)
    one2 = jnp.where(e == i2, 1.0, 0.0)
    both_slots = one1 + one2
    before = jnp.dot(tri_ref[...], both_slots.astype(BF16), preferred_element_type=F32) + run_ref[...]
    r1 = jnp.sum(one1 * before, axis=1, keepdims=True)
    r2 = jnp.sum(one2 * before, axis=1, keepdims=True)
    rank_ref[...] = jnp.concatenate([r1, r2], axis=1).astype(jnp.int32)
    run_ref[...] += jnp.sum(both_slots, axis=0, keepdims=True)
    cnt_ref[...] = run_ref[...].astype(jnp.int32)


def _router(h, wr):
    m, d = h.shape
    tm = _divisor_tile(m, 1024, V7X_BF16_SUBLANES)
    w_hi = wr.astype(BF16)
    w2 = jnp.concatenate([w_hi, (wr - w_hi.astype(F32)).astype(BF16)], axis=1)
    per_token = lambda: pl.BlockSpec((tm, 2), lambda i: (i, 0))
    return pl.pallas_call(
        _router_body,
        grid=(m // tm,),
        in_specs=[pl.BlockSpec((tm, d), lambda i: (i, 0)), _resident((d, 2 * N_EXPERTS))],
        out_specs=[per_token(), per_token(), per_token(), pl.BlockSpec((1, N_EXPERTS), lambda i: (0, 0))],
        out_shape=[jax.ShapeDtypeStruct((m, 2), jnp.int32), jax.ShapeDtypeStruct((m, 2), F32),
                   jax.ShapeDtypeStruct((m, 2), jnp.int32), jax.ShapeDtypeStruct((1, N_EXPERTS), jnp.int32)],
        scratch_shapes=[pltpu.VMEM((tm, tm), BF16), pltpu.VMEM((1, N_EXPERTS), F32)],
        compiler_params=_params("arbitrary"),
        name="router",
    )(h, w2)


def _index_prefetch(idx_hbm, idx_smem, isem, i, n):
    width = idx_hbm.shape[1]
    slot = lax.rem(i, 2)
    half = lambda sl: idx_smem.at[pl.ds(pl.multiple_of(sl * width, width), width)]

    @pl.when(i == 0)
    def _():
        pltpu.make_async_copy(idx_hbm.at[0], half(0), isem.at[0]).start()

    pltpu.make_async_copy(idx_hbm.at[i], half(slot), isem.at[slot]).wait()

    @pl.when(i + 1 < n)
    def _():
        pltpu.make_async_copy(idx_hbm.at[i + 1], half(1 - slot), isem.at[1 - slot]).start()

    return slot * width


def _slab_at(off):
    return pl.ds(pl.multiple_of(off, SLAB), SLAB)


def _slab_rows(p):
    return pl.ds(pl.multiple_of(p * SLAB, SLAB), SLAB)


def _slabs_to_matrix(ref2, rows):
    return jnp.concatenate([ref2[pl.ds(g, rows, stride=SLAB), :] for g in range(SLAB)], axis=1)


def _matrix_to_slabs(x, ref2, row0=0):
    for g in range(SLAB):
        ref2[pl.ds(row0 * SLAB + g, x.shape[0], stride=SLAB), :] = x[:, g * V7X_LANES:(g + 1) * V7X_LANES]


def _dispatch_body(st_ref, cnt_ref, pd_ref, pos_hbm, h3_ref, xs_hbm, idx_smem, zbuf, isem, sem, psem,
                   *, tm, n_rows):
    i = pl.program_id(0)
    n = pl.num_programs(0)
    base = _index_prefetch(pos_hbm, idx_smem, isem, i, n)

    def issue(r, carry):
        src = h3_ref.at[_slab_rows(r)]
        pltpu.make_async_copy(src, xs_hbm.at[_slab_at(idx_smem[base + r])], sem).start()
        pltpu.make_async_copy(src, xs_hbm.at[_slab_at(idx_smem[base + tm + r])], sem).start(priority=1)
        return carry

    lax.fori_loop(0, tm, issue, 0, unroll=8)
    pltpu.make_async_copy(h3_ref, xs_hbm.at[pl.ds(0, tm * SLAB)], sem).wait()
    pltpu.make_async_copy(h3_ref, xs_hbm.at[pl.ds(0, tm * SLAB)], sem).wait()

    @pl.when(i == n - 1)
    def _():
        zbuf[...] = jnp.zeros_like(zbuf)
        zrow = zbuf.at[pl.ds(0, SLAB)]
        for e in range(N_EXPERTS):
            lo = st_ref[e] + cnt_ref[e]
            npad = pd_ref[e] - cnt_ref[e]

            def fill(r, carry):
                pltpu.make_async_copy(zrow, xs_hbm.at[_slab_rows(lo + r)], psem).start()
                return carry

            def drain(r, carry):
                pltpu.make_async_copy(zrow, xs_hbm.at[_slab_rows(lo + r)], psem).wait()
                return carry

            lax.fori_loop(0, npad, fill, 0)
            lax.fori_loop(0, npad, drain, 0)

        zr = zbuf.shape[0] // SLAB
        tail = st_ref[N_EXPERTS - 1] + pd_ref[N_EXPERTS - 1]
        nblk = (n_rows - tail) // zr

        def block(k):
            return xs_hbm.at[pl.ds(pl.multiple_of((tail + k * zr) * SLAB, SLAB), zr * SLAB)]

        def fill_block(k, carry):
            pltpu.make_async_copy(zbuf, block(k), psem).start()
            return carry

        def drain_block(k, carry):
            pltpu.make_async_copy(zbuf, block(k), psem).wait()
            return carry

        lax.fori_loop(0, nblk, fill_block, 0)
        lax.fori_loop(0, nblk, drain_block, 0)


def _dispatch(h3, pos2d, starts, counts, padded, n_rows, tm):
    m = h3.shape[0] // SLAB
    grid_spec = pltpu.PrefetchScalarGridSpec(
        num_scalar_prefetch=3,
        grid=(m // tm,),
        in_specs=[pl.BlockSpec(memory_space=pl.ANY),
                  pl.BlockSpec((tm * SLAB, V7X_LANES), lambda i, *_: (i, 0))],
        out_specs=pl.BlockSpec(memory_space=pl.ANY),
        scratch_shapes=[pltpu.SMEM((4 * tm,), jnp.int32), pltpu.VMEM((ZERO_ROWS * SLAB, V7X_LANES), F32),
                        pltpu.SemaphoreType.DMA((2,)), pltpu.SemaphoreType.DMA, pltpu.SemaphoreType.DMA],
    )
    return pl.pallas_call(
        functools.partial(_dispatch_body, tm=tm, n_rows=n_rows),
        grid_spec=grid_spec,
        out_shape=jax.ShapeDtypeStruct((n_rows * SLAB, V7X_LANES), F32),
        compiler_params=_params("arbitrary"),
        name="moe_dispatch",
    )(starts, counts, padded, pos2d, h3)


def _moe_body(te_ref, nu_ref, x_ref, wg_ref, wu_ref, wd_ref, o_ref, hid_ref):
    @pl.when(pl.program_id(0) < nu_ref[0])
    def _():
        xb = _slabs_to_matrix(x_ref, hid_ref.shape[0]).astype(BF16)
        _swiglu_hidden(xb, wg_ref, wu_ref, hid_ref, widx=(0, 0))
        for rs in _row_blocks(hid_ref.shape[0]):
            _matrix_to_slabs(jnp.dot(hid_ref[rs, :], wd_ref[0, 0], preferred_element_type=F32), o_ref, rs.start)

    @pl.when(pl.program_id(0) >= nu_ref[0])
    def _():
        o_ref[...] = jnp.zeros_like(o_ref)


def _moe_experts(xs, tile_expert, n_used, wg, wu, wd, layer, tme):
    r = xs.shape[0] // SLAB
    d, dff = wg.shape[2], wg.shape[3]
    assert dff % V7X_MXU_COLS == 0
    expert = lambda shape: pl.BlockSpec((1, 1) + shape, lambda i, te, nu: (layer, te[i], 0, 0),
                                        pipeline_mode=pl.Buffered(1))
    grid_spec = pltpu.PrefetchScalarGridSpec(
        num_scalar_prefetch=2,
        grid=(r // tme,),
        in_specs=[pl.BlockSpec((tme * SLAB, V7X_LANES), lambda i, te, nu: (i, 0)),
                  expert((d, dff)), expert((d, dff)), expert((dff, d))],
        out_specs=pl.BlockSpec((tme * SLAB, V7X_LANES), lambda i, te, nu: (i, 0)),
        scratch_shapes=[pltpu.VMEM((tme, dff), BF16)],
    )
    return pl.pallas_call(
        _moe_body,
        grid_spec=grid_spec,
        out_shape=jax.ShapeDtypeStruct(xs.shape, F32),
        compiler_params=_params("arbitrary"),
        name="moe_experts",
    )(tile_expert, n_used, xs, wg, wu, wd)


def _combine_body(pos_hbm, ys_hbm, wt_ref, h_ref, g_ref, b_ref, o_ref, ybuf, idx_smem, isem, sem, *, tm, alpha):
    i = pl.program_id(0)
    base = _index_prefetch(pos_hbm, idx_smem, isem, i, pl.num_programs(0))

    def issue(r, carry):
        pltpu.make_async_copy(ys_hbm.at[_slab_at(idx_smem[base + r])], ybuf.at[0, _slab_rows(r)], sem).start()
        pltpu.make_async_copy(ys_hbm.at[_slab_at(idx_smem[base + tm + r])], ybuf.at[1, _slab_rows(r)],
                              sem).start(priority=1)
        return carry

    lax.fori_loop(0, tm, issue, 0, unroll=8)
    pltpu.make_async_copy(ys_hbm.at[pl.ds(0, tm * SLAB)], ybuf.at[0], sem).wait()
    pltpu.make_async_copy(ys_hbm.at[pl.ds(0, tm * SLAB)], ybuf.at[1], sem).wait()
    for rs in _row_blocks(tm):
        n = rs.stop - rs.start
        rows = lambda k: ybuf.at[k, pl.ds(rs.start * SLAB, n * SLAB)]
        wt = wt_ref[rs, :]
        out = wt[:, 0:1] * _slabs_to_matrix(rows(0), n) + wt[:, 1:2] * _slabs_to_matrix(rows(1), n)
        o_ref[rs, :] = _layer_norm(alpha * h_ref[rs, :] + out, g_ref[...], b_ref[...])


def _combine(pos2d, ys, wt, h, g, b, alpha, tm):
    m, d = h.shape
    return pl.pallas_call(
        functools.partial(_combine_body, tm=tm, alpha=alpha),
        grid=(m // tm,),
        in_specs=[pl.BlockSpec(memory_space=pl.ANY), pl.BlockSpec(memory_space=pl.ANY),
                  pl.BlockSpec((tm, 2), lambda i: (i, 0)),
                  pl.BlockSpec((tm, d), lambda i: (i, 0)),
                  pl.BlockSpec((1, d), lambda i: (0, 0)), pl.BlockSpec((1, d), lambda i: (0, 0))],
        out_specs=pl.BlockSpec((tm, d), lambda i: (i, 0)),
        out_shape=jax.ShapeDtypeStruct((m, d), F32),
        scratch_shapes=[pltpu.VMEM((2, tm * SLAB, V7X_LANES), F32), pltpu.SMEM((4 * tm,), jnp.int32),
                        pltpu.SemaphoreType.DMA((2,)), pltpu.SemaphoreType.DMA],
        compiler_params=_params("arbitrary"),
        name="moe_combine",
    )(pos2d, ys, wt, h, g, b)


def _moe_layer(h, h_slabs, wr, wg, wu, wd, layer, g, b, alpha):
    m, d = h.shape
    assert d == SLAB * V7X_LANES
    tme = _divisor_tile(m, 1024, ZERO_ROWS)
    tmc = _divisor_tile(m, 1024, 8)
    idx, wts, rank, counts = _router(h, wr)

    counts = counts[0]
    padded = ((counts + tme - 1) // tme) * tme
    ends = jnp.cumsum(padded)
    starts = ends - padded
    pos = jnp.sum(jnp.where(idx[:, :, None] == jnp.arange(N_EXPERTS, dtype=jnp.int32), starts, 0), axis=-1) + rank
    n_rows = 2 * m + N_EXPERTS * tme
    n_tiles = n_rows // tme
    tile_start = jnp.arange(n_tiles, dtype=jnp.int32) * tme
    tile_expert = jnp.minimum(jnp.sum(tile_start[:, None] >= ends[None, :], axis=1), N_EXPERTS - 1).astype(jnp.int32)
    n_used = (ends[-1] // tme).astype(jnp.int32).reshape(1)
    pos2d = (pos * SLAB).reshape(m // tmc, tmc, 2).transpose(0, 2, 1).reshape(m // tmc, 2 * tmc)

    xs = _dispatch(h_slabs, pos2d, starts, counts, padded, n_rows, tmc)
    ys = _moe_experts(xs, tile_expert, n_used, wg, wu, wd, layer, tme)
    return _combine(pos2d, ys, wts, h, g, b, alpha, tmc)


def kernel(x, meta, w_in, b_gate, gla_w_a2, gla_b_a, gla_norm_g, ret_norm_g, ret_norm_b, w_pa, w_pb, w_o,
           ln1_g, ln1_b, ffn_w_gate, ffn_w_up, ffn_w_down, moe_w_router, moe_w_gate, moe_w_up, moe_w_down,
           ln2_g, ln2_b):
    bsz, seq, d = x.shape
    depth = w_in.shape[0]
    alpha = (2 * depth) ** 0.25
    lp = seq + N_META + PAD
    assert lp % CHUNK == 0 and d == VW
    m = bsz * lp

    h = jnp.concatenate([jnp.zeros((bsz, PAD, d), x.dtype),
                         jnp.broadcast_to(meta[None].astype(x.dtype), (bsz, N_META, d)), x], axis=1)
    h = h.reshape(m, d)
    tables = _retention_tables(lp)
    keep = jnp.asarray(np.tile((np.arange(lp) >= PAD).astype(np.float32), bsz).reshape(m, 1))

    c0 = 2 * QK + 2 * VW
    row = lambda a: a.reshape(1, -1)
    moe_wg, moe_wu, moe_wd = moe_w_gate.astype(BF16), moe_w_up.astype(BF16), moe_w_down.astype(BF16)
    ffn_wg, ffn_wu, ffn_wd = ffn_w_gate.astype(BF16), ffn_w_up.astype(BF16), ffn_w_down.astype(BF16)
    wpa_all, wpb_all, wo_all = w_pa.astype(BF16), w_pb.astype(BF16), w_o.astype(BF16)
    w_perm = jnp.concatenate(
        [w_in[:, :, :c0], w_in[:, :, c0 + GLA_RANK:], w_in[:, :, c0:c0 + GLA_RANK],
         jnp.zeros((depth, d, V7X_MXU_COLS - GLA_RANK), w_in.dtype)], axis=2).astype(BF16)
    for l in range(depth):
        wa2 = jnp.concatenate([gla_w_a2[l], jnp.zeros((V7X_LANES - GLA_RANK, QK), F32)], axis=0).astype(BF16)

        proj = _in_proj(h, keep, w_perm, l)
        ya, yb = _mixer(proj.reshape(bsz, lp, N_PROJ), wa2, row(gla_b_a[l]), row(gla_norm_g[l]),
                        row(ret_norm_g[l]), row(ret_norm_b[l]), tables)
        is_moe = l % 2 == 1
        mixed = _mix_out(ya.reshape(m, VW), yb.reshape(m, VW), proj, h,
                         wpa_all, wpb_all, wo_all, l,
                         row(b_gate[l, 0]), row(b_gate[l, 1]), row(ln1_g[l]), row(ln1_b[l]), alpha, is_moe)
        h = mixed[0]
        if l % 2 == 0:
            j = l // 2
            h = _ffn(h, ffn_wg, ffn_wu, ffn_wd, j, row(ln2_g[l]), row(ln2_b[l]), alpha)
        else:
            j = l // 2
            h = _moe_layer(h, mixed[1], moe_w_router[j], moe_wg, moe_wu, moe_wd, j,
                           row(ln2_g[l]), row(ln2_b[l]), alpha)
    return h.reshape(bsz, lp, d)[:, PAD + N_META:]
```
